```python
import math
import jax, jax.numpy as jnp
from jax import lax
import numpy as np

D_MODEL = 2048
BATCH = 4
SEQ = 2048
DEPTH = 2
DEC_BATCH = 32
DEC_SEQ = 4
PAST_LEN = 8192
PAGE_SIZE = 128

N_MIXERS = 2
M_HEADS = 8
M_DQK = D_MODEL // (2 * M_HEADS)
M_DV = D_MODEL // M_HEADS
M_CHUNK = 64
F_HEADS = 16
F_DH = D_MODEL // F_HEADS
Q_BLOCK = 128
D_FF = 5504
ALPHA = (2 * DEPTH) ** 0.25
BETA = (8 * DEPTH) ** -0.25
LN_EPS = 1e-5
NORM_EPS = 1e-6

kernel_name = "mlstm_fox_macaron_deepnorm_step"


def _rms(x, w):
    x32 = x.astype(jnp.float32)
    return x32 * lax.rsqrt(jnp.mean(x32 * x32, axis=-1, keepdims=True) + NORM_EPS) * w.astype(jnp.float32)


def _post_ln(res, upd, g, b):
    z = ALPHA * res.astype(jnp.float32) + upd.astype(jnp.float32)
    mu = jnp.mean(z, axis=-1, keepdims=True)
    zc = z - mu
    var = jnp.mean(zc * zc, axis=-1, keepdims=True)
    return (zc * lax.rsqrt(var + LN_EPS) * g.astype(jnp.float32) + b.astype(jnp.float32)).astype(res.dtype)


def _swiglu(x, w_in, w_out):
    gu = jnp.einsum("btd,df->btf", x, w_in)
    g, u = jnp.split(gu, 2, axis=-1)
    return jnp.einsum("btf,fd->btd", jax.nn.silu(g) * u, w_out)


def _mlstm_in(x, m_w_in, m_b_i, m_b_f):
    bsz, t, _ = x.shape
    nqk = M_HEADS * M_DQK
    nv = M_HEADS * M_DV
    z = jnp.einsum("btd,de->bte", x, m_w_in).astype(jnp.float32)
    q, k, v, o, ig, fg = jnp.split(z, [nqk, 2 * nqk, 2 * nqk + nv, 2 * nqk + 2 * nv, 2 * nqk + 2 * nv + M_HEADS], axis=-1)
    q = q.reshape(bsz, t, M_HEADS, M_DQK).transpose(0, 2, 1, 3)
    k = k.reshape(bsz, t, M_HEADS, M_DQK).transpose(0, 2, 1, 3) * (M_DQK ** -0.5)
    v = v.reshape(bsz, t, M_HEADS, M_DV).transpose(0, 2, 1, 3)
    ig = (ig + m_b_i.astype(jnp.float32)).transpose(0, 2, 1)
    lf = jax.nn.log_sigmoid(fg + m_b_f.astype(jnp.float32)).transpose(0, 2, 1)
    return q, k, v, o, ig, lf


def _mlstm_chunk(carry, inp):
    c_st, n_st, m_st = carry
    q, k, v, ig, lf = inp
    L = q.shape[2]
    b = jnp.cumsum(lf, axis=-1)
    dmat = b[..., :, None] - b[..., None, :] + ig[..., None, :]
    causal = jnp.tril(jnp.ones((L, L), dtype=bool))
    dmat = jnp.where(causal, dmat, -jnp.inf)
    inter = b + m_st[..., None]
    m_t = jnp.maximum(jnp.max(dmat, axis=-1), inter)
    w_intra = jnp.exp(dmat - m_t[..., None])
    w_inter = jnp.exp(inter - m_t)
    s = jnp.einsum("bhtk,bhsk->bhts", q, k) * w_intra
    num = w_inter[..., None] * jnp.einsum("bhvk,bhtk->bhtv", c_st, q) + jnp.einsum("bhts,bhsv->bhtv", s, v)
    den = w_inter * jnp.einsum("bhk,bhtk->bht", n_st, q) + jnp.sum(s, axis=-1)
    h = num / jnp.maximum(jnp.abs(den), jnp.exp(-m_t))[..., None]
    dec_s = b[..., -1:] - b + ig
    m_new = jnp.maximum(b[..., -1] + m_st, jnp.max(dec_s, axis=-1))
    w_s = jnp.exp(dec_s - m_new[..., None])
    w_c = jnp.exp(b[..., -1] + m_st - m_new)
    c_new = w_c[..., None, None] * c_st + jnp.einsum("bhsv,bhsk->bhvk", v * w_s[..., None], k)
    n_new = w_c[..., None] * n_st + jnp.einsum("bhs,bhsk->bhk", w_s, k)
    return (c_new, n_new, m_new), h


def _mlstm_out(h, o, m_norm, m_w_out, dtype):
    bsz, _, t, _ = h.shape
    hn = _rms(h.transpose(0, 2, 1, 3), m_norm.reshape(M_HEADS, M_DV))
    hn = hn.reshape(bsz, t, M_HEADS * M_DV) * jax.nn.sigmoid(o)
    return jnp.einsum("bte,ed->btd", hn.astype(dtype), m_w_out)


def _mlstm_prompt(x, m_w_in, m_b_i, m_b_f, m_norm, m_w_out):
    bsz, t, _ = x.shape
    q, k, v, o, ig, lf = _mlstm_in(x, m_w_in, m_b_i, m_b_f)
    nc = t // M_CHUNK

    def chunks(a):
        a = a.reshape(a.shape[:2] + (nc, M_CHUNK) + a.shape[3:])
        return jnp.moveaxis(a, 2, 0)

    init = (jnp.zeros((bsz, M_HEADS, M_DV, M_DQK), jnp.float32),
            jnp.zeros((bsz, M_HEADS, M_DQK), jnp.float32),
            jnp.zeros((bsz, M_HEADS), jnp.float32))
    state, h = lax.scan(_mlstm_chunk, init, (chunks(q), chunks(k), chunks(v), chunks(ig), chunks(lf)))
    h = jnp.moveaxis(h, 0, 2).reshape(bsz, M_HEADS, t, M_DV)
    return _mlstm_out(h, o, m_norm, m_w_out, x.dtype), state


def _mlstm_sample(x, state_C, state_n, state_m, m_w_in, m_b_i, m_b_f, m_norm, m_w_out):
    q, k, v, o, ig, lf = _mlstm_in(x, m_w_in, m_b_i, m_b_f)
    init = (state_C.astype(jnp.float32), state_n.astype(jnp.float32), state_m.astype(jnp.float32))
    state, h = _mlstm_chunk(init, (q, k, v, ig, lf))
    return _mlstm_out(h, o, m_norm, m_w_out, x.dtype), state


def _fox_in(x, f_w_in, f_b_f, f_q_norm, f_k_norm):
    bsz, t, _ = x.shape
    z = jnp.einsum("btd,de->bte", x, f_w_in)
    q, k, v, g, fg = jnp.split(z, [D_MODEL, 2 * D_MODEL, 3 * D_MODEL, 4 * D_MODEL], axis=-1)
    q = _rms(q.reshape(bsz, t, F_HEADS, F_DH), f_q_norm)
    k = _rms(k.reshape(bsz, t, F_HEADS, F_DH), f_k_norm)
    v = v.reshape(bsz, t, F_HEADS, F_DH).astype(jnp.float32)
    lf = jax.nn.log_sigmoid(fg.astype(jnp.float32) + f_b_f.astype(jnp.float32))
    return q, k, v, g, lf


def _fox_out(o, g, f_w_out, dtype):
    bsz, t = o.shape[:2]
    y = o.reshape(bsz, t, D_MODEL) * jax.nn.sigmoid(g.astype(jnp.float32))
    return jnp.einsum("bte,ed->btd", y.astype(dtype), f_w_out)


def _fox_prompt(x, f_w_in, f_b_f, f_q_norm, f_k_norm, f_w_out):
    bsz, t, _ = x.shape
    q, k, v, g, lf = _fox_in(x, f_w_in, f_b_f, f_q_norm, f_k_norm)
    qh, kh, vh = (a.transpose(0, 2, 1, 3) for a in (q, k, v))
    c = jnp.cumsum(lf, axis=1).transpose(0, 2, 1)
    kpos = jnp.arange(t)
    scale = F_DH ** -0.5

    def block(i):
        start = i * Q_BLOCK
        qb = lax.dynamic_slice_in_dim(qh, start, Q_BLOCK, axis=2)
        cb = lax.dynamic_slice_in_dim(c, start, Q_BLOCK, axis=2)
        logits = jnp.einsum("bhqd,bhkd->bhqk", qb, kh) * scale + cb[..., :, None] - c[..., None, :]
        qpos = start + jnp.arange(Q_BLOCK)
        logits = jnp.where(qpos[:, None] >= kpos[None, :], logits, -jnp.inf)
        return jnp.einsum("bhqk,bhkd->bhqd", jax.nn.softmax(logits, axis=-1), vh)

    o = lax.map(block, jnp.arange(t // Q_BLOCK))
    o = o.transpose(1, 0, 3, 2, 4).reshape(bsz, t, F_HEADS, F_DH)
    return _fox_out(o, g, f_w_out, x.dtype), (k, v, lf)


def _fox_sample(x, cache_k, cache_v, cache_logf, page_table, f_w_in, f_b_f, f_q_norm, f_k_norm, f_w_out):
    bsz, t, _ = x.shape
    n_pages = page_table.shape[1]
    q, k, v, g, lf = _fox_in(x, f_w_in, f_b_f, f_q_norm, f_k_norm)
    scale = F_DH ** -0.5
    qh = q.transpose(0, 2, 1, 3)
    cn = jnp.cumsum(lf, axis=1).transpose(0, 2, 1)
    logits = jnp.einsum("bhqd,bkhd->bhqk", qh, k) * scale + cn[..., :, None] - cn[..., None, :]
    logits = jnp.where(jnp.tril(jnp.ones((t, t), dtype=bool)), logits, -jnp.inf)
    mx = jnp.max(logits, axis=-1)
    p = jnp.exp(logits - mx[..., None])
    den = jnp.sum(p, axis=-1)
    acc = jnp.einsum("bhqk,bkhd->bhqd", p, v)
    lf_past = cache_logf[page_table].astype(jnp.float32).reshape(bsz, n_pages * PAGE_SIZE, F_HEADS)
    r = lax.cumsum(lf_past, axis=1, reverse=True) - lf_past
    r = r.reshape(bsz, n_pages, PAGE_SIZE, F_HEADS).transpose(1, 0, 3, 2)

    def page_step(carry, inp):
        mx, den, acc = carry
        pid, rp = inp
        kp = cache_k[pid].astype(jnp.float32)
        vp = cache_v[pid].astype(jnp.float32)
        lg = jnp.einsum("bhqd,bkhd->bhqk", qh, kp) * scale + cn[..., :, None] + rp[..., None, :]
        m_new = jnp.maximum(mx, jnp.max(lg, axis=-1))
        corr = jnp.exp(mx - m_new)
        pp = jnp.exp(lg - m_new[..., None])
        return (m_new, den * corr + jnp.sum(pp, axis=-1),
                acc * corr[..., None] + jnp.einsum("bhqk,bkhd->bhqd", pp, vp)), None

    (mx, den, acc), _ = lax.scan(page_step, (mx, den, acc), (page_table.T, r))
    o = (acc / den[..., None]).transpose(0, 2, 1, 3)
    return _fox_out(o, g, f_w_out, x.dtype), (k, v, lf)


def setup_inputs(seed: int = 0) -> dict:
    key = jax.random.key(seed)
    ks = jax.random.split(key, 24)
    nrm = jax.random.normal
    n_pages = PAST_LEN // PAGE_SIZE
    n_used = DEC_BATCH * n_pages
    n_phys = n_used + max(n_used // 4, 1)
    nqk = M_HEADS * M_DQK
    nv = M_HEADS * M_DV
    d_in_m = 2 * nqk + 2 * nv + 2 * M_HEADS
    d_in_f = 4 * D_MODEL + F_HEADS

    x_prompt = nrm(ks[0], (BATCH, SEQ, D_MODEL), jnp.float32)
    x_sample = nrm(ks[1], (DEC_BATCH, DEC_SEQ, D_MODEL), jnp.float32)
    state_C = 0.5 * nrm(ks[2], (DEC_BATCH, M_HEADS, M_DV, M_DQK), jnp.float32)
    state_n = nrm(ks[3], (DEC_BATCH, M_HEADS, M_DQK), jnp.float32)
    state_m = 0.5 * nrm(ks[4], (DEC_BATCH, M_HEADS), jnp.float32)
    f_b_f = jnp.linspace(1.0, 6.0, F_HEADS, dtype=jnp.float32) + 0.1 * nrm(ks[5], (F_HEADS,), jnp.float32)
    cache_k = nrm(ks[6], (n_phys, PAGE_SIZE, F_HEADS, F_DH), jnp.float32)
    cache_v = 0.5 * nrm(ks[7], (n_phys, PAGE_SIZE, F_HEADS, F_DH), jnp.float32)
    cache_logf = jax.nn.log_sigmoid(nrm(ks[8], (n_phys, PAGE_SIZE, F_HEADS), jnp.float32) + f_b_f)
    page_table = jax.random.permutation(ks[9], n_phys)[:n_used].reshape(DEC_BATCH, n_pages).astype(jnp.int32)

    ln_g = 1.0 + 0.02 * nrm(ks[10], (DEPTH, 3, D_MODEL), jnp.float32)
    ln_b = 0.02 * nrm(ks[11], (DEPTH, 3, D_MODEL), jnp.float32)
    ffn_w_in = nrm(ks[12], (DEPTH, 2, D_MODEL, 2 * D_FF), jnp.float32) * D_MODEL ** -0.5
    ffn_w_out = nrm(ks[13], (DEPTH, 2, D_FF, D_MODEL), jnp.float32) * (D_FF ** -0.5 * BETA)
    m_w_in = nrm(ks[14], (D_MODEL, d_in_m), jnp.float32) * D_MODEL ** -0.5
    m_b_i = -1.0 + 0.1 * nrm(ks[15], (M_HEADS,), jnp.float32)
    m_b_f = jnp.linspace(3.0, 6.0, M_HEADS, dtype=jnp.float32) + 0.1 * nrm(ks[16], (M_HEADS,), jnp.float32)
    m_norm = 1.0 + 0.02 * nrm(ks[17], (nv,), jnp.float32)
    m_w_out = nrm(ks[18], (nv, D_MODEL), jnp.float32) * (nv ** -0.5 * BETA)
    f_w_in = nrm(ks[19], (D_MODEL, d_in_f), jnp.float32) * D_MODEL ** -0.5
    f_q_norm = 1.0 + 0.02 * nrm(ks[20], (F_DH,), jnp.float32)
    f_k_norm = 1.0 + 0.02 * nrm(ks[21], (F_DH,), jnp.float32)
    f_w_out = nrm(ks[22], (D_MODEL, D_MODEL), jnp.float32) * (D_MODEL ** -0.5 * BETA)
    return {"x_prompt": x_prompt, "x_sample": x_sample,
            "state_C": state_C, "state_n": state_n, "state_m": state_m,
            "cache_k": cache_k, "cache_v": cache_v, "cache_logf": cache_logf, "page_table": page_table,
            "ln_g": ln_g, "ln_b": ln_b, "ffn_w_in": ffn_w_in, "ffn_w_out": ffn_w_out,
            "m_w_in": m_w_in, "m_b_i": m_b_i, "m_b_f": m_b_f, "m_norm": m_norm, "m_w_out": m_w_out,
            "f_w_in": f_w_in, "f_b_f": f_b_f, "f_q_norm": f_q_norm, "f_k_norm": f_k_norm, "f_w_out": f_w_out}


def reference(x_prompt, x_sample, state_C, state_n, state_m, cache_k, cache_v, cache_logf, page_table,
              ln_g, ln_b, ffn_w_in, ffn_w_out, m_w_in, m_b_i, m_b_f, m_norm, m_w_out,
              f_w_in, f_b_f, f_q_norm, f_k_norm, f_w_out):
    xp, xs = x_prompt, x_sample
    for layer in range(DEPTH):
        g, b = ln_g[layer], ln_b[layer]
        xp = _post_ln(xp, 0.5 * _swiglu(xp, ffn_w_in[layer, 0], ffn_w_out[layer, 0]), g[0], b[0])
        xs = _post_ln(xs, 0.5 * _swiglu(xs, ffn_w_in[layer, 0], ffn_w_out[layer, 0]), g[0], b[0])
        if layer % N_MIXERS == 0:
            yp, (c_p, n_p, m_p) = _mlstm_prompt(xp, m_w_in, m_b_i, m_b_f, m_norm, m_w_out)
            ys, (c_s, n_s, m_s) = _mlstm_sample(xs, state_C, state_n, state_m, m_w_in, m_b_i, m_b_f, m_norm, m_w_out)
        else:
            yp, (k_p, v_p, lf_p) = _fox_prompt(xp, f_w_in, f_b_f, f_q_norm, f_k_norm, f_w_out)
            ys, (k_s, v_s, lf_s) = _fox_sample(xs, cache_k, cache_v, cache_logf, page_table,
                                               f_w_in, f_b_f, f_q_norm, f_k_norm, f_w_out)
        xp = _post_ln(xp, yp, g[1], b[1])
        xs = _post_ln(xs, ys, g[1], b[1])
        xp = _post_ln(xp, 0.5 * _swiglu(xp, ffn_w_in[layer, 1], ffn_w_out[layer, 1]), g[2], b[2])
        xs = _post_ln(xs, 0.5 * _swiglu(xs, ffn_w_in[layer, 1], ffn_w_out[layer, 1]), g[2], b[2])
    return (xp, xs, c_p, n_p, m_p, c_s, n_s, m_s, k_p, v_p, lf_p, k_s, v_s, lf_s)
```

```python
import functools
import math

import jax
import jax.numpy as jnp
from jax import lax
from jax.experimental import pallas as pl
from jax.experimental.pallas import tpu as pltpu

F32 = jnp.float32
BF16 = jnp.bfloat16
LN_EPS = 1e-5
NORM_EPS = 1e-6
LANES = 128
CHUNK = 128
PAGE = 128
FF_ALIGN = 512
VMEM_LIMIT = 56 * 1024 * 1024
NT_DIMS = (((1,), (1,)), ((), ()))


def _cparams(*sem):
    return pltpu.CompilerParams(dimension_semantics=sem, vmem_limit_bytes=VMEM_LIMIT)


def _pick(n, cands):
    for c in cands:
        if c <= n and n % c == 0:
            return c
    return n


def _log_sigmoid(x):
    return jnp.minimum(x, 0.0) - jnp.log1p(jnp.exp(-jnp.abs(x)))


def _dot01(x, m01):
    hi = x.astype(BF16)
    r1 = x - hi.astype(F32)
    mid = r1.astype(BF16)
    lo = (r1 - mid.astype(F32)).astype(BF16)
    d = lambda a: jnp.dot(a, m01, preferred_element_type=F32)
    return d(hi) + d(mid) + d(lo)


def _rows_to_cols(rows):
    r = rows.shape[0]
    if r < LANES:
        rows = jnp.concatenate([rows, jnp.zeros((LANES - r, LANES), F32)], axis=0)
    return rows.T


def _ffn_in_body(x_ref, wg_ref, wu_ref, h_ref):
    x = x_ref[...]
    g = jnp.dot(x, wg_ref[...], preferred_element_type=F32)
    u = jnp.dot(x, wu_ref[...], preferred_element_type=F32)
    h_ref[...] = (g * jax.nn.sigmoid(g) * u).astype(h_ref.dtype)


def _ffn_in(x16, wg, wu):
    n, d = x16.shape
    f = wg.shape[1]
    tm = _pick(n, (832, 640, 528, 512, 256, 128))
    tf = _pick(f, (512, 256, 128))
    return pl.pallas_call(
        _ffn_in_body,
        grid=(n // tm, f // tf),
        in_specs=[pl.BlockSpec((tm, d), lambda i, j: (i, 0)),
                  pl.BlockSpec((d, tf), lambda i, j: (0, j)),
                  pl.BlockSpec((d, tf), lambda i, j: (0, j))],
        out_specs=pl.BlockSpec((tm, tf), lambda i, j: (i, j)),
        out_shape=jax.ShapeDtypeStruct((n, f), BF16),
        compiler_params=_cparams("parallel", "arbitrary"),
        name="ffn_in",
    )(x16, wg, wu)


def _mm_res_ln_body(a_ref, w_ref, res_ref, g_ref, b_ref, o32_ref, o16_ref, acc_ref, *, alpha, scale, nk):
    k = pl.program_id(1)
    part = jnp.dot(a_ref[...], w_ref[...], preferred_element_type=F32)

    @pl.when(k == 0)
    def _():
        acc_ref[...] = part

    @pl.when(k > 0)
    def _():
        acc_ref[...] += part

    @pl.when(k == nk - 1)
    def _():
        z = alpha * res_ref[...] + scale * acc_ref[...]
        mu = jnp.mean(z, axis=-1, keepdims=True)
        zc = z - mu
        var = jnp.mean(zc * zc, axis=-1, keepdims=True)
        y = zc * lax.rsqrt(var + LN_EPS) * g_ref[...] + b_ref[...]
        o32_ref[...] = y
        o16_ref[...] = y.astype(BF16)


def _mm_res_ln(a16, w16, res32, g, b, *, alpha, scale):
    n, kdim = a16.shape
    d = w16.shape[1]
    tm = _pick(n, (640, 512, 256, 128))
    tk = _pick(kdim, (512, 256, 128))
    nk = kdim // tk
    body = functools.partial(_mm_res_ln_body, alpha=alpha, scale=scale, nk=nk)
    return pl.pallas_call(
        body,
        grid=(n // tm, nk),
        in_specs=[pl.BlockSpec((tm, tk), lambda i, k: (i, k)),
                  pl.BlockSpec((tk, d), lambda i, k: (k, 0)),
                  pl.BlockSpec((tm, d), lambda i, k: (i, 0)),
                  pl.BlockSpec((1, d), lambda i, k: (0, 0)),
                  pl.BlockSpec((1, d), lambda i, k: (0, 0))],
        out_specs=[pl.BlockSpec((tm, d), lambda i, k: (i, 0)),
                   pl.BlockSpec((tm, d), lambda i, k: (i, 0))],
        out_shape=[jax.ShapeDtypeStruct((n, d), F32), jax.ShapeDtypeStruct((n, d), BF16)],
        scratch_shapes=[pltpu.VMEM((tm, d), F32)],
        compiler_params=_cparams("parallel", "arbitrary"),
        name="mm_res_ln",
    )(a16, w16, res32, g.reshape(1, d), b.reshape(1, d))


def _proj_body(x_ref, w_ref, cs_ref, o_ref, *, n_norm_tiles, group):
    z = jnp.dot(x_ref[...], w_ref[...], preferred_element_type=F32)
    cs = cs_ref[...]
    if n_norm_tiles == 0:
        o_ref[...] = z * cs
        return
    j = pl.program_id(1)

    @pl.when(j >= n_norm_tiles)
    def _():
        o_ref[...] = z * cs

    @pl.when(j < n_norm_tiles)
    def _():
        for s in range(z.shape[1] // group):
            sl = slice(s * group, (s + 1) * group)
            zh = z[:, sl]
            ms = jnp.mean(zh * zh, axis=-1, keepdims=True)
            o_ref[:, sl] = zh * lax.rsqrt(ms + NORM_EPS) * cs[:, sl]


def _proj(x16, w16, colscale, *, norm_cols=0, group=LANES):
    n, d = x16.shape
    e = w16.shape[1]
    tm = _pick(n, (832, 640, 528, 512, 256, 128))
    tn = _pick(e, (512, 256, 128))
    body = functools.partial(_proj_body, n_norm_tiles=norm_cols // tn, group=group)
    return pl.pallas_call(
        body,
        grid=(n // tm, e // tn),
        in_specs=[pl.BlockSpec((tm, d), lambda i, j: (i, 0)),
                  pl.BlockSpec((d, tn), lambda i, j: (0, j)),
                  pl.BlockSpec((1, tn), lambda i, j: (0, j))],
        out_specs=pl.BlockSpec((tm, tn), lambda i, j: (i, j)),
        out_shape=jax.ShapeDtypeStruct((n, e), F32),
        compiler_params=_cparams("parallel", "arbitrary"),
        name="proj",
    )(x16, w16, colscale.reshape(1, e))


def _gate_body(x_ref, wc_ref, wr_ref, bc_ref, br_ref, oc_ref, or_ref, *, ls_from, n_gates):
    x = x_ref[...]
    zc = jnp.dot(x, wc_ref[...], preferred_element_type=F32) + bc_ref[...]
    lane = lax.broadcasted_iota(jnp.int32, zc.shape, 1)
    oc_ref[...] = jnp.where((lane >= ls_from) & (lane < n_gates), _log_sigmoid(zc), zc)
    zr = lax.dot_general(wr_ref[...], x, NT_DIMS, preferred_element_type=F32) + br_ref[...]
    row = lax.broadcasted_iota(jnp.int32, zr.shape, 0)
    or_ref[...] = jnp.where(row >= ls_from, _log_sigmoid(zr), zr)


def _gates(x16, w_gate, bias, *, ls_from):
    n, d = x16.shape
    ng = w_gate.shape[1]
    tm = _pick(n, (1664, 1280, 1024, 640, 512, 256, 128))
    wc = jnp.zeros((d, LANES), BF16).at[:, :ng].set(w_gate.astype(BF16))
    wr = w_gate.astype(BF16).T
    bc = jnp.zeros((1, LANES), F32).at[0, :ng].set(bias)
    br = bias.reshape(ng, 1)
    body = functools.partial(_gate_body, ls_from=ls_from, n_gates=ng)
    return pl.pallas_call(
        body,
        grid=(n // tm,),
        in_specs=[pl.BlockSpec((tm, d), lambda i: (i, 0)),
                  pl.BlockSpec((d, LANES), lambda i: (0, 0)),
                  pl.BlockSpec((ng, d), lambda i: (0, 0)),
                  pl.BlockSpec((1, LANES), lambda i: (0, 0)),
                  pl.BlockSpec((ng, 1), lambda i: (0, 0))],
        out_specs=[pl.BlockSpec((tm, LANES), lambda i: (i, 0)),
                   pl.BlockSpec((ng, tm), lambda i: (0, i))],
        out_shape=[jax.ShapeDtypeStruct((n, LANES), F32), jax.ShapeDtypeStruct((ng, n), F32)],
        compiler_params=_cparams("parallel"),
        name="gates",
    )(x16, wc, wr, bc, br)


def _mlstm_body(q_ref, k_ref, v_ref, o_ref, g_ref, c0_ref, n0_ref, m0_ref, nw_ref,
                h_ref, c_ref, n_ref, m_ref, *, heads, dqk, dv, seg_shift, per_batch_seg):
    b = pl.program_id(0)
    c = pl.program_id(1)
    L = CHUNK
    seg_len = 1 << seg_shift
    seg = b if per_batch_seg else 0

    @pl.when(c == 0)
    def _():
        c_ref[...] = c0_ref[...]
        n_ref[...] = n0_ref[...]
        m_ref[...] = m0_ref[...]

    if per_batch_seg:
        @pl.when(b == 0)
        def _():
            h_ref[...] = jnp.zeros(h_ref.shape, h_ref.dtype)

    li = lax.broadcasted_iota(jnp.int32, (1, L), 1)
    si = lax.broadcasted_iota(jnp.int32, (L, 1), 0)
    lseg = lax.shift_right_logical(li, seg_shift)
    sseg = lax.shift_right_logical(si, seg_shift)
    pair_ok = (lseg == sseg) & (li <= si)
    row_live = sseg == seg
    lane_live = lseg == seg
    last_lane = seg * seg_len + (seg_len - 1)
    cum01 = jnp.where((lax.shift_right_logical(si, seg_shift) == lseg) & (si <= li), 1.0, 0.0).astype(BF16)

    gates = g_ref[...]
    ig = gates[:heads]
    lf = gates[heads:]
    bcum = _dot01(lf, cum01)
    src = ig - bcum
    bcols = _rows_to_cols(bcum)

    for h in range(heads):
        b_col = bcols[:, h:h + 1]
        b_row = bcum[h:h + 1, :]
        m_prev = m_ref[h:h + 1, 0:1]
        dmat = jnp.where(pair_ok, b_col + src[h:h + 1, :], -jnp.inf)
        inter = b_col + m_prev
        m_t = jnp.maximum(jnp.max(dmat, axis=-1, keepdims=True), inter)
        w_intra = jnp.exp(dmat - m_t)
        w_inter = jnp.exp(inter - m_t)

        qh = q_ref[:, h * dqk:(h + 1) * dqk]
        q16 = qh.astype(BF16)
        k16 = k_ref[:, h * dqk:(h + 1) * dqk].astype(BF16)
        vh = v_ref[:, h * dv:(h + 1) * dv]
        v16 = vh.astype(BF16)
        c_st = c_ref[h]
        n_st = n_ref[h:h + 1, :]

        s = lax.dot_general(q16, k16, NT_DIMS, preferred_element_type=F32) * w_intra
        num = (w_inter * lax.dot_general(q16, c_st.astype(BF16), NT_DIMS, preferred_element_type=F32)
               + jnp.dot(s.astype(BF16), v16, preferred_element_type=F32))
        den = (w_inter * jnp.sum(qh * n_st, axis=-1, keepdims=True)
               + jnp.sum(s, axis=-1, keepdims=True))
        hv = num / jnp.maximum(jnp.abs(den), jnp.exp(-m_t))
        ms = jnp.mean(hv * hv, axis=-1, keepdims=True)
        hn = hv * lax.rsqrt(ms + NORM_EPS) * nw_ref[:, h * dv:(h + 1) * dv]
        out = (hn * jax.nn.sigmoid(o_ref[:, h * dv:(h + 1) * dv])).astype(h_ref.dtype)
        if per_batch_seg:
            out = jnp.where(row_live, out, h_ref[:, h * dv:(h + 1) * dv])
        h_ref[:, h * dv:(h + 1) * dv] = out

        b_last = jnp.sum(jnp.where(li == last_lane, b_row, 0.0), axis=-1, keepdims=True)
        dec = jnp.where(lane_live, b_last - b_row + ig[h:h + 1, :], -jnp.inf)
        m_new = jnp.maximum(b_last + m_prev, jnp.max(dec, axis=-1, keepdims=True))
        w_s = jnp.exp(dec - m_new)
        w_c = jnp.exp(b_last + m_prev - m_new)
        vw = (vh.T * w_s).astype(BF16)
        c_ref[h] = w_c * c_st + jnp.dot(vw, k16, preferred_element_type=F32)
        w_s8 = jnp.broadcast_to(w_s, (8, L)).astype(BF16)
        n_ref[h:h + 1, :] = w_c * n_st + jnp.dot(w_s8, k16, preferred_element_type=F32)[0:1]
        m_ref[h:h + 1, :] = jnp.broadcast_to(m_new, (1, LANES))


def _mlstm(z, grow, c0, n0, m0, norm_w, *, row_block0, n_chunks, seg_shift, per_batch_seg, heads, dqk, dv):
    nb = c0.shape[0]
    nqk = heads * dqk
    nv = heads * dv
    qk_blocks_before_v = (2 * nqk) // nv
    if per_batch_seg:
        rb = lambda b, c: row_block0
        n_rows = CHUNK
        out_rb = lambda b, c: 0
    else:
        rb = lambda b, c: row_block0 + b * n_chunks + c
        n_rows = nb * n_chunks * CHUNK
        out_rb = lambda b, c: b * n_chunks + c
    body = functools.partial(_mlstm_body, heads=heads, dqk=dqk, dv=dv, seg_shift=seg_shift,
                             per_batch_seg=per_batch_seg)
    return pl.pallas_call(
        body,
        grid=(nb, n_chunks),
        in_specs=[pl.BlockSpec((CHUNK, nqk), lambda b, c: (rb(b, c), 0)),
                  pl.BlockSpec((CHUNK, nqk), lambda b, c: (rb(b, c), 1)),
                  pl.BlockSpec((CHUNK, nv), lambda b, c: (rb(b, c), qk_blocks_before_v)),
                  pl.BlockSpec((CHUNK, nv), lambda b, c: (rb(b, c), qk_blocks_before_v + 1)),
                  pl.BlockSpec((2 * heads, CHUNK), lambda b, c: (0, rb(b, c))),
                  pl.BlockSpec((None, heads, dv, dqk), lambda b, c: (b, 0, 0, 0)),
                  pl.BlockSpec((None, heads, dqk), lambda b, c: (b, 0, 0)),
                  pl.BlockSpec((None, heads, LANES), lambda b, c: (b, 0, 0)),
                  pl.BlockSpec((1, nv), lambda b, c: (0, 0))],
        out_specs=[pl.BlockSpec((CHUNK, nv), lambda b, c: (out_rb(b, c), 0)),
                   pl.BlockSpec((None, heads, dv, dqk), lambda b, c: (b, 0, 0, 0)),
                   pl.BlockSpec((None, heads, dqk), lambda b, c: (b, 0, 0)),
                   pl.BlockSpec((None, heads, LANES), lambda b, c: (b, 0, 0))],
        out_shape=[jax.ShapeDtypeStruct((n_rows, nv), BF16),
                   jax.ShapeDtypeStruct((nb, heads, dv, dqk), F32),
                   jax.ShapeDtypeStruct((nb, heads, dqk), F32),
                   jax.ShapeDtypeStruct((nb, heads, LANES), F32)],
        compiler_params=_cparams("arbitrary", "arbitrary"),
        name="mlstm_sample" if per_batch_seg else "mlstm_prompt",
    )(z, z, z, z, grow, c0, n0, m0, norm_w.reshape(1, nv))


def _cumsum_body(lf_ref, o_ref):
    t = lf_ref.shape[1]
    si = lax.broadcasted_iota(jnp.int32, (LANES, LANES), 0)
    li = lax.broadcasted_iota(jnp.int32, (LANES, LANES), 1)
    incl = jnp.where(si <= li, 1.0, 0.0).astype(BF16)
    carry = jnp.zeros((lf_ref.shape[0], 1), F32)
    for r in range(t // LANES):
        blk = _dot01(lf_ref[:, r * LANES:(r + 1) * LANES], incl) + carry
        o_ref[:, r * LANES:(r + 1) * LANES] = blk
        carry = blk[:, LANES - 1:LANES]


def _cumsum_rows(lf_rows, nb, t):
    hh = lf_rows.shape[0]
    return pl.pallas_call(
        _cumsum_body,
        grid=(nb,),
        in_specs=[pl.BlockSpec((hh, t), lambda b: (0, b))],
        out_specs=pl.BlockSpec((None, hh, t), lambda b: (b, 0, 0)),
        out_shape=jax.ShapeDtypeStruct((nb, hh, t), F32),
        compiler_params=_cparams("parallel"),
        name="fox_cumsum",
    )(lf_rows)


def _fox_attn_body(q_ref, k_ref, v_ref, g_ref, cq_ref, ck_ref, o_ref, m_s, l_s, acc_s, cqb_s, *, tq, tk, scale):
    qi = pl.program_id(2)
    ki = pl.program_id(3)
    rep = tk // LANES

    @pl.when(ki == 0)
    def _():
        m_s[...] = jnp.full(m_s.shape, -jnp.inf, F32)
        l_s[...] = jnp.zeros(l_s.shape, F32)
        acc_s[...] = jnp.zeros(acc_s.shape, F32)
        for r in range(tq // LANES):
            row = cq_ref[:, r * LANES:(r + 1) * LANES]
            cqb_s[r * LANES:(r + 1) * LANES, :] = jnp.broadcast_to(row, (LANES, LANES)).T

    def step(masked):
        q16 = q_ref[...].astype(BF16)
        k16 = k_ref[...].astype(BF16)
        s = lax.dot_general(q16, k16, NT_DIMS, preferred_element_type=F32) * scale
        s = s + jnp.concatenate([cqb_s[...]] * rep, axis=1) - ck_ref[...]
        if masked:
            qpos = qi * tq + lax.broadcasted_iota(jnp.int32, (tq, tk), 0)
            kpos = ki * tk + lax.broadcasted_iota(jnp.int32, (tq, tk), 1)
            s = jnp.where(qpos >= kpos, s, -jnp.inf)
        m_prev = m_s[...]
        m_new = jnp.maximum(m_prev, jnp.max(s, axis=-1, keepdims=True))
        p = jnp.exp(s - jnp.concatenate([m_new] * rep, axis=1))
        corr = jnp.exp(m_prev - m_new)
        l_s[...] = corr * l_s[...] + jnp.sum(p, axis=-1, keepdims=True)
        acc_s[...] = corr * acc_s[...] + jnp.dot(p.astype(BF16), v_ref[...].astype(BF16),
                                                  preferred_element_type=F32)
        m_s[...] = m_new

    @pl.when(ki < qi)
    def _():
        step(False)

    @pl.when(ki == qi)
    def _():
        step(True)
        o = acc_s[...] / l_s[...]
        o_ref[...] = (o * jax.nn.sigmoid(g_ref[...])).astype(o_ref.dtype)


def _fox_attn(z, c3, *, nb, t, heads, dh):
    tq = tk = _pick(t, (512, 256, 128))
    nq = t // tq
    scale = dh ** -0.5
    body = functools.partial(_fox_attn_body, tq=tq, tk=tk, scale=scale)
    kmap = lambda b, h, qi, ki: jnp.minimum(ki, qi)
    return pl.pallas_call(
        body,
        grid=(nb, heads, nq, nq),
        in_specs=[pl.BlockSpec((tq, dh), lambda b, h, qi, ki: (b * nq + qi, h)),
                  pl.BlockSpec((tk, dh), lambda b, h, qi, ki: (b * nq + kmap(b, h, qi, ki), heads + h)),
                  pl.BlockSpec((tk, dh), lambda b, h, qi, ki: (b * nq + kmap(b, h, qi, ki), 2 * heads + h)),
                  pl.BlockSpec((tq, dh), lambda b, h, qi, ki: (b * nq + qi, 3 * heads + h)),
                  pl.BlockSpec((None, 1, tq), lambda b, h, qi, ki: (b * heads + h, 0, qi)),
                  pl.BlockSpec((None, 1, tk), lambda b, h, qi, ki: (b * heads + h, 0, kmap(b, h, qi, ki)))],
        out_specs=pl.BlockSpec((tq, dh), lambda b, h, qi, ki: (b * nq + qi, h)),
        out_shape=jax.ShapeDtypeStruct((nb * t, heads * dh), BF16),
        scratch_shapes=[pltpu.VMEM((tq, LANES), F32), pltpu.VMEM((tq, LANES), F32),
                        pltpu.VMEM((tq, dh), F32), pltpu.VMEM((tq, LANES), F32)],
        compiler_params=_cparams("parallel", "parallel", "parallel", "arbitrary"),
        name="fox_attn",
    )(z, z, z, z, c3, c3)


def _fox_decode_body(pt_ref, q_ref, kn_ref, vn_ref, g_ref, lfn_ref, kp_ref, vp_ref, lfp_ref, o_ref,
                     q_s, m_s, l_s, acc_s, carry_s, cn_s, kn_s, vn_s, *, heads, dh, n_new, n_pages, scale):
    p = pl.program_id(1)
    li = lax.broadcasted_iota(jnp.int32, (1, PAGE), 1)
    si8 = lax.broadcasted_iota(jnp.int32, (8, 1), 0)

    def attend(k_page, v_page, bias_rows, ok):
        for h in range(heads):
            sl = slice(h * dh, (h + 1) * dh)
            s = lax.dot_general(q_s[:, sl].astype(BF16), k_page[:, sl].astype(BF16), NT_DIMS, preferred_element_type=F32)
            lg = s * scale + cn_s[:, h:h + 1] + bias_rows[h:h + 1, :]
            if ok is not None:
                lg = jnp.where(ok, lg, -jnp.inf)
            m_prev = m_s[:, sl]
            m_new = jnp.maximum(m_prev, jnp.max(lg, axis=-1, keepdims=True))
            pp = jnp.exp(lg - m_new)
            corr = jnp.exp(m_prev - m_new)
            l_s[:, sl] = corr * l_s[:, sl] + jnp.sum(pp, axis=-1, keepdims=True)
            acc_s[:, sl] = corr * acc_s[:, sl] + jnp.dot(pp.astype(BF16), v_page[:, sl].astype(BF16),
                                                         preferred_element_type=F32)
            m_s[:, sl] = m_new

    @pl.when(p == 0)
    def _():
        q_s[...] = jnp.zeros(q_s.shape, F32)
        q_s[0:n_new, :] = q_ref[...]
        m_s[...] = jnp.full(m_s.shape, -jnp.inf, F32)
        l_s[...] = jnp.zeros(l_s.shape, F32)
        acc_s[...] = jnp.zeros(acc_s.shape, F32)
        carry_s[...] = jnp.zeros(carry_s.shape, F32)
        cn_s[...] = jnp.zeros(cn_s.shape, F32)
        run = lfn_ref[0:1, :]
        cn_s[0:1, :] = run
        for t in range(1, n_new):
            run = run + lfn_ref[t:t + 1, :]
            cn_s[t:t + 1, :] = run
        kn_s[...] = jnp.zeros(kn_s.shape, F32)
        vn_s[...] = jnp.zeros(vn_s.shape, F32)
        kn_s[0:n_new, :] = kn_ref[...]
        vn_s[0:n_new, :] = vn_ref[...]
        bias = -_rows_to_cols(cn_s[...])[:heads]
        ok = (li <= si8) & (li < n_new)
        attend(kn_s[...], vn_s[...], bias, ok)

    lfp = lfp_ref[...]
    sj = lax.broadcasted_iota(jnp.int32, (PAGE, PAGE), 0)
    ls = lax.broadcasted_iota(jnp.int32, (PAGE, PAGE), 1)
    after01 = jnp.where(sj > ls, 1.0, 0.0).astype(BF16)
    ones01 = jnp.ones((PAGE, PAGE), BF16)
    r = _dot01(lfp, after01) + carry_s[...]
    attend(kp_ref[...], vp_ref[...], r, None)
    carry_s[...] = carry_s[...] + _dot01(lfp, ones01)

    @pl.when(p == n_pages - 1)
    def _():
        o = acc_s[0:n_new, :] / l_s[0:n_new, :]
        o_ref[...] = (o * jax.nn.sigmoid(g_ref[...])).astype(o_ref.dtype)


def _fox_decode(zs, lf_cols_s, cache_k, cache_v, cache_lft, page_table, *, heads, dh, n_new):
    nb, n_pages = page_table.shape
    d_all = heads * dh
    scale = dh ** -0.5
    body = functools.partial(_fox_decode_body, heads=heads, dh=dh, n_new=n_new, n_pages=n_pages, scale=scale)
    page = lambda b, p, pt: pt[b, n_pages - 1 - p]
    grid_spec = pltpu.PrefetchScalarGridSpec(
        num_scalar_prefetch=1,
        grid=(nb, n_pages),
        in_specs=[pl.BlockSpec((None, n_new, d_all), lambda b, p, pt: (b, 0, 0)),
                  pl.BlockSpec((None, n_new, d_all), lambda b, p, pt: (b, 0, 1)),
                  pl.BlockSpec((None, n_new, d_all), lambda b, p, pt: (b, 0, 2)),
                  pl.BlockSpec((None, n_new, d_all), lambda b, p, pt: (b, 0, 3)),
                  pl.BlockSpec((None, n_new, LANES), lambda b, p, pt: (b, 0, 0)),
                  pl.BlockSpec((None, PAGE, d_all), lambda b, p, pt: (page(b, p, pt), 0, 0)),
                  pl.BlockSpec((None, PAGE, d_all), lambda b, p, pt: (page(b, p, pt), 0, 0)),
                  pl.BlockSpec((None, heads, PAGE), lambda b, p, pt: (page(b, p, pt), 0, 0))],
        out_specs=pl.BlockSpec((None, n_new, d_all), lambda b, p, pt: (b, 0, 0)),
        scratch_shapes=[pltpu.VMEM((8, d_all), F32), pltpu.VMEM((8, d_all), F32),
                        pltpu.VMEM((8, d_all), F32), pltpu.VMEM((8, d_all), F32),
                        pltpu.VMEM((heads, PAGE), F32), pltpu.VMEM((8, LANES), F32),
                        pltpu.VMEM((PAGE, d_all), F32), pltpu.VMEM((PAGE, d_all), F32)],
    )
    return pl.pallas_call(
        body,
        grid_spec=grid_spec,
        out_shape=jax.ShapeDtypeStruct((nb, n_new, d_all), BF16),
        compiler_params=_cparams("parallel", "arbitrary"),
        name="fox_decode",
    )(page_table, zs, zs, zs, zs, lf_cols_s, cache_k, cache_v, cache_lft)


def _ffn_weights(w_in, w_out):
    d, f2 = w_in.shape
    f = f2 // 2
    fp = -(-f // FF_ALIGN) * FF_ALIGN
    wg = jnp.pad(w_in[:, :f].astype(BF16), ((0, 0), (0, fp - f)))
    wu = jnp.pad(w_in[:, f:].astype(BF16), ((0, 0), (0, fp - f)))
    wo = jnp.pad(w_out.astype(BF16), ((0, fp - f), (0, 0)))
    return wg, wu, wo


def kernel(x_prompt, x_sample, state_C, state_n, state_m, cache_k, cache_v, cache_logf, page_table,
           ln_g, ln_b, ffn_w_in, ffn_w_out, m_w_in, m_b_i, m_b_f, m_norm, m_w_out,
           f_w_in, f_b_f, f_q_norm, f_k_norm, f_w_out):
    nb_p, t, d = x_prompt.shape
    nb_s, t_s, _ = x_sample.shape
    depth = ln_g.shape[0]
    alpha = (2 * depth) ** 0.25
    n_p = nb_p * t
    n_s = nb_s * t_s
    assert n_s == CHUNK and t % CHUNK == 0 and (t_s & (t_s - 1)) == 0
    m_heads = m_b_i.shape[0]
    m_dqk = d // (2 * m_heads)
    m_dv = d // m_heads
    f_heads = f_b_f.shape[0]
    f_dh = d // f_heads
    nqk = m_heads * m_dqk
    nv = m_heads * m_dv

    x32 = jnp.concatenate([x_prompt.reshape(n_p, d), x_sample.reshape(n_s, d)], axis=0)
    x16 = x32.astype(BF16)

    def ffn(x32, x16, layer, which, g, b):
        wg, wu, wo = _ffn_weights(ffn_w_in[layer, which], ffn_w_out[layer, which])
        hid = _ffn_in(x16, wg, wu)
        return _mm_res_ln(hid, wo, x32, g, b, alpha=alpha, scale=0.5)

    outs = {}
    for layer in range(depth):
        g, b = ln_g[layer], ln_b[layer]
        x32, x16 = ffn(x32, x16, layer, 0, g[0], b[0])

        if layer % 2 == 0:
            n_main = 2 * nqk + 2 * nv
            colscale = jnp.concatenate([jnp.ones((nqk,), F32), jnp.full((nqk,), m_dqk ** -0.5, F32),
                                        jnp.ones((2 * nv,), F32)])
            z = _proj(x16, m_w_in[:, :n_main].astype(BF16), colscale)
            _, grow = _gates(x16, m_w_in[:, n_main:], jnp.concatenate([m_b_i, m_b_f]), ls_from=m_heads)
            common = dict(heads=m_heads, dqk=m_dqk, dv=m_dv)
            zero_state = (jnp.zeros((nb_p, m_heads, m_dv, m_dqk), F32), jnp.zeros((nb_p, m_heads, m_dqk), F32),
                          jnp.zeros((nb_p, m_heads, LANES), F32))
            hp, c_p, n_pr, m_p = _mlstm(z, grow, *zero_state, m_norm, row_block0=0, n_chunks=t // CHUNK,
                                        seg_shift=int(math.log2(CHUNK)), per_batch_seg=False, **common)
            m0_s = jnp.broadcast_to(state_m.astype(F32)[:, :, None], (nb_s, m_heads, LANES))
            hs, c_s, n_sm, m_s = _mlstm(z, grow, state_C.astype(F32), state_n.astype(F32), m0_s, m_norm,
                                        row_block0=n_p // CHUNK, n_chunks=1, seg_shift=int(math.log2(t_s)),
                                        per_batch_seg=True, **common)
            y16 = jnp.concatenate([hp, hs], axis=0)
            w_mix = m_w_out
            outs.update(c_p=c_p, n_p=n_pr, m_p=m_p[:, :, 0], c_s=c_s, n_s=n_sm, m_s=m_s[:, :, 0])
        else:
            n_main = 4 * d
            colscale = jnp.concatenate([jnp.tile(f_q_norm.astype(F32), f_heads), jnp.tile(f_k_norm.astype(F32), f_heads),
                                        jnp.ones((2 * d,), F32)])
            z = _proj(x16, f_w_in[:, :n_main].astype(BF16), colscale, norm_cols=2 * d, group=f_dh)
            lf_cols, lf_rows = _gates(x16, f_w_in[:, n_main:], f_b_f, ls_from=0)
            c3 = _cumsum_rows(lf_rows, nb_p, t).reshape(nb_p * f_heads, 1, t)
            yp = _fox_attn(z, c3, nb=nb_p, t=t, heads=f_heads, dh=f_dh)
            n_phys = cache_k.shape[0]
            ys = _fox_decode(z[n_p:].reshape(nb_s, t_s, n_main), lf_cols[n_p:].reshape(nb_s, t_s, LANES),
                             cache_k.reshape(n_phys, PAGE, d), cache_v.reshape(n_phys, PAGE, d),
                             jnp.swapaxes(cache_logf, 1, 2), page_table,
                             heads=f_heads, dh=f_dh, n_new=t_s)
            y16 = jnp.concatenate([yp, ys.reshape(n_s, d)], axis=0)
            w_mix = f_w_out
            kk = z[:, d:2 * d]
            vv = z[:, 2 * d:3 * d]
            lf = lf_cols[:, :f_heads]
            outs.update(k_p=kk[:n_p].reshape(nb_p, t, f_heads, f_dh), v_p=vv[:n_p].reshape(nb_p, t, f_heads, f_dh),
                        lf_p=lf[:n_p].reshape(nb_p, t, f_heads),
                        k_s=kk[n_p:].reshape(nb_s, t_s, f_heads, f_dh), v_s=vv[n_p:].reshape(nb_s, t_s, f_heads, f_dh),
                        lf_s=lf[n_p:].reshape(nb_s, t_s, f_heads))

        x32, x16 = _mm_res_ln(y16, w_mix.astype(BF16), x32, g[1], b[1], alpha=alpha, scale=1.0)
        x32, x16 = ffn(x32, x16, layer, 1, g[2], b[2])

    xp = x32[:n_p].reshape(nb_p, t, d)
    xs = x32[n_p:].reshape(nb_s, t_s, d)
    return (xp, xs, outs["c_p"], outs["n_p"], outs["m_p"], outs["c_s"], outs["n_s"], outs["m_s"],
            outs["k_p"], outs["v_p"], outs["lf_p"], outs["k_s"], outs["v_s"], outs["lf_s"])
```

```python
import functools
import math

import jax
import jax.numpy as jnp
from jax import lax
from jax.experimental import pallas as pl
from jax.experimental.pallas import tpu as pltpu

F32 = jnp.float32
BF16 = jnp.bfloat16
LN_EPS = 1e-5
NORM_EPS = 1e-6
LANES = 128
CHUNK = 128
PAGE = 128
FF_ALIGN = 512
VMEM_LIMIT = 56 * 1024 * 1024
NT_DIMS = (((1,), (1,)), ((), ()))


def _cparams(*sem):
    return pltpu.CompilerParams(dimension_semantics=sem, vmem_limit_bytes=VMEM_LIMIT)


def _pick(n, cands):
    for c in cands:
        if c <= n and n % c == 0:
            return c
    return n


def _log_sigmoid(x):
    return jnp.minimum(x, 0.0) - jnp.log1p(jnp.exp(-jnp.abs(x)))


def _dot01(x, m01):
    hi = x.astype(BF16)
    r1 = x - hi.astype(F32)
    mid = r1.astype(BF16)
    lo = (r1 - mid.astype(F32)).astype(BF16)
    d = lambda a: jnp.dot(a, m01, preferred_element_type=F32)
    return d(hi) + d(mid) + d(lo)


def _rows_to_cols(rows):
    r = rows.shape[0]
    if r < LANES:
        rows = jnp.concatenate([rows, jnp.zeros((LANES - r, LANES), F32)], axis=0)
    return rows.T


def _ffn_in_body(x_ref, wg_ref, wu_ref, h_ref):
    x = x_ref[...]
    g = jnp.dot(x, wg_ref[...], preferred_element_type=F32)
    u = jnp.dot(x, wu_ref[...], preferred_element_type=F32)
    h_ref[...] = (g * jax.nn.sigmoid(g) * u).astype(h_ref.dtype)


def _ffn_in(x16, wg, wu):
    n, d = x16.shape
    f = wg.shape[1]
    tm = _pick(n, (832, 640, 528, 512, 256, 128))
    tf = _pick(f, (512, 256, 128))
    return pl.pallas_call(
        _ffn_in_body,
        grid=(n // tm, f // tf),
        in_specs=[pl.BlockSpec((tm, d), lambda i, j: (i, 0)),
                  pl.BlockSpec((d, tf), lambda i, j: (0, j)),
                  pl.BlockSpec((d, tf), lambda i, j: (0, j))],
        out_specs=pl.BlockSpec((tm, tf), lambda i, j: (i, j)),
        out_shape=jax.ShapeDtypeStruct((n, f), BF16),
        compiler_params=_cparams("parallel", "arbitrary"),
        name="ffn_in",
    )(x16, wg, wu)


def _mm_res_ln_body(a_ref, w_ref, res_ref, g_ref, b_ref, o32_ref, o16_ref, *, alpha, scale):
    acc = jnp.dot(a_ref[...], w_ref[...], preferred_element_type=F32)
    z = alpha * res_ref[...] + scale * acc
    mu = jnp.mean(z, axis=-1, keepdims=True)
    zc = z - mu
    var = jnp.mean(zc * zc, axis=-1, keepdims=True)
    y = zc * lax.rsqrt(var + LN_EPS) * g_ref[...] + b_ref[...]
    o32_ref[...] = y
    o16_ref[...] = y.astype(BF16)


def _mm_res_ln(a16, w16, res32, g, b, *, alpha, scale):
    n, kdim = a16.shape
    d = w16.shape[1]
    tm = _pick(n, (320, 256, 128))
    body = functools.partial(_mm_res_ln_body, alpha=alpha, scale=scale)
    return pl.pallas_call(
        body,
        grid=(n // tm,),
        in_specs=[pl.BlockSpec((tm, kdim), lambda i: (i, 0)),
                  pl.BlockSpec((kdim, d), lambda i: (0, 0), pipeline_mode=pl.Buffered(1)),
                  pl.BlockSpec((tm, d), lambda i: (i, 0)),
                  pl.BlockSpec((1, d), lambda i: (0, 0)),
                  pl.BlockSpec((1, d), lambda i: (0, 0))],
        out_specs=[pl.BlockSpec((tm, d), lambda i: (i, 0)),
                   pl.BlockSpec((tm, d), lambda i: (i, 0))],
        out_shape=[jax.ShapeDtypeStruct((n, d), F32), jax.ShapeDtypeStruct((n, d), BF16)],
        compiler_params=_cparams("parallel"),
        name="mm_res_ln",
    )(a16, w16, res32, g.reshape(1, d), b.reshape(1, d))


def _proj_body(x_ref, w_ref, cs_ref, o_ref, *, n_norm_tiles, group):
    z = jnp.dot(x_ref[...], w_ref[...], preferred_element_type=F32)
    cs = cs_ref[...]
    if n_norm_tiles == 0:
        o_ref[...] = z * cs
        return
    j = pl.program_id(1)

    @pl.when(j >= n_norm_tiles)
    def _():
        o_ref[...] = z * cs

    @pl.when(j < n_norm_tiles)
    def _():
        for s in range(z.shape[1] // group):
            sl = slice(s * group, (s + 1) * group)
            zh = z[:, sl]
            ms = jnp.mean(zh * zh, axis=-1, keepdims=True)
            o_ref[:, sl] = zh * lax.rsqrt(ms + NORM_EPS) * cs[:, sl]


def _proj(x16, w16, colscale, *, norm_cols=0, group=LANES):
    n, d = x16.shape
    e = w16.shape[1]
    tm = _pick(n, (832, 640, 528, 512, 256, 128))
    tn = _pick(e, (512, 256, 128))
    body = functools.partial(_proj_body, n_norm_tiles=norm_cols // tn, group=group)
    return pl.pallas_call(
        body,
        grid=(n // tm, e // tn),
        in_specs=[pl.BlockSpec((tm, d), lambda i, j: (i, 0)),
                  pl.BlockSpec((d, tn), lambda i, j: (0, j)),
                  pl.BlockSpec((1, tn), lambda i, j: (0, j))],
        out_specs=pl.BlockSpec((tm, tn), lambda i, j: (i, j)),
        out_shape=jax.ShapeDtypeStruct((n, e), F32),
        compiler_params=_cparams("parallel", "arbitrary"),
        name="proj",
    )(x16, w16, colscale.reshape(1, e))


def _gate_body(x_ref, wc_ref, wr_ref, bc_ref, br_ref, oc_ref, or_ref, *, ls_from, n_gates):
    x = x_ref[...]
    zc = jnp.dot(x, wc_ref[...], preferred_element_type=F32) + bc_ref[...]
    lane = lax.broadcasted_iota(jnp.int32, zc.shape, 1)
    oc_ref[...] = jnp.where((lane >= ls_from) & (lane < n_gates), _log_sigmoid(zc), zc)
    zr = lax.dot_general(wr_ref[...], x, NT_DIMS, preferred_element_type=F32) + br_ref[...]
    row = lax.broadcasted_iota(jnp.int32, zr.shape, 0)
    or_ref[...] = jnp.where(row >= ls_from, _log_sigmoid(zr), zr)


def _gates(x16, w_gate, bias, *, ls_from):
    n, d = x16.shape
    ng = w_gate.shape[1]
    tm = _pick(n, (1664, 1280, 1024, 640, 512, 256, 128))
    wc = jnp.zeros((d, LANES), BF16).at[:, :ng].set(w_gate.astype(BF16))
    wr = w_gate.astype(BF16).T
    bc = jnp.zeros((1, LANES), F32).at[0, :ng].set(bias)
    br = bias.reshape(ng, 1)
    body = functools.partial(_gate_body, ls_from=ls_from, n_gates=ng)
    return pl.pallas_call(
        body,
        grid=(n // tm,),
        in_specs=[pl.BlockSpec((tm, d), lambda i: (i, 0)),
                  pl.BlockSpec((d, LANES), lambda i: (0, 0)),
                  pl.BlockSpec((ng, d), lambda i: (0, 0)),
                  pl.BlockSpec((1, LANES), lambda i: (0, 0)),
                  pl.BlockSpec((ng, 1), lambda i: (0, 0))],
        out_specs=[pl.BlockSpec((tm, LANES), lambda i: (i, 0)),
                   pl.BlockSpec((ng, tm), lambda i: (0, i))],
        out_shape=[jax.ShapeDtypeStruct((n, LANES), F32), jax.ShapeDtypeStruct((ng, n), F32)],
        compiler_params=_cparams("parallel"),
        name="gates",
    )(x16, wc, wr, bc, br)


def _mlstm_body(q_ref, k_ref, v_ref, o_ref, g_ref, c0_ref, n0_ref, m0_ref, nw_ref,
                h_ref, c_ref, n_ref, m_ref, *, heads, dqk, dv, seg_shift, per_batch_seg):
    b = pl.program_id(0)
    c = pl.program_id(1)
    L = CHUNK
    seg_len = 1 << seg_shift
    seg = b if per_batch_seg else 0

    @pl.when(c == 0)
    def _():
        c_ref[...] = c0_ref[...]
        n_ref[...] = n0_ref[...]
        m_ref[...] = m0_ref[...]

    if per_batch_seg:
        @pl.when(b == 0)
        def _():
            h_ref[...] = jnp.zeros(h_ref.shape, h_ref.dtype)

    li = lax.broadcasted_iota(jnp.int32, (1, L), 1)
    si = lax.broadcasted_iota(jnp.int32, (L, 1), 0)
    lseg = lax.shift_right_logical(li, seg_shift)
    sseg = lax.shift_right_logical(si, seg_shift)
    pair_ok = (lseg == sseg) & (li <= si)
    row_live = sseg == seg
    lane_live = lseg == seg
    last_lane = seg * seg_len + (seg_len - 1)
    cum01 = jnp.where((lax.shift_right_logical(si, seg_shift) == lseg) & (si <= li), 1.0, 0.0).astype(BF16)

    gates = g_ref[...]
    ig = gates[:heads]
    lf = gates[heads:]
    bcum = _dot01(lf, cum01)
    src = ig - bcum
    bcols = _rows_to_cols(bcum)

    for h in range(heads):
        b_col = bcols[:, h:h + 1]
        b_row = bcum[h:h + 1, :]
        m_prev = m_ref[h:h + 1, 0:1]
        dmat = jnp.where(pair_ok, b_col + src[h:h + 1, :], -jnp.inf)
        inter = b_col + m_prev
        m_t = jnp.maximum(jnp.max(dmat, axis=-1, keepdims=True), inter)
        w_intra = jnp.exp(dmat - m_t)
        w_inter = jnp.exp(inter - m_t)

        qh = q_ref[:, h * dqk:(h + 1) * dqk]
        q16 = qh.astype(BF16)
        k16 = k_ref[:, h * dqk:(h + 1) * dqk].astype(BF16)
        vh = v_ref[:, h * dv:(h + 1) * dv]
        v16 = vh.astype(BF16)
        c_st = c_ref[h]
        n_st = n_ref[h:h + 1, :]

        s = lax.dot_general(q16, k16, NT_DIMS, preferred_element_type=F32) * w_intra
        num = (w_inter * lax.dot_general(q16, c_st.astype(BF16), NT_DIMS, preferred_element_type=F32)
               + jnp.dot(s.astype(BF16), v16, preferred_element_type=F32))
        den = (w_inter * jnp.sum(qh * n_st, axis=-1, keepdims=True)
               + jnp.sum(s, axis=-1, keepdims=True))
        hv = num / jnp.maximum(jnp.abs(den), jnp.exp(-m_t))
        ms = jnp.mean(hv * hv, axis=-1, keepdims=True)
        hn = hv * lax.rsqrt(ms + NORM_EPS) * nw_ref[:, h * dv:(h + 1) * dv]
        out = (hn * jax.nn.sigmoid(o_ref[:, h * dv:(h + 1) * dv])).astype(h_ref.dtype)
        if per_batch_seg:
            out = jnp.where(row_live, out, h_ref[:, h * dv:(h + 1) * dv])
        h_ref[:, h * dv:(h + 1) * dv] = out

        b_last = jnp.sum(jnp.where(li == last_lane, b_row, 0.0), axis=-1, keepdims=True)
        dec = jnp.where(lane_live, b_last - b_row + ig[h:h + 1, :], -jnp.inf)
        m_new = jnp.maximum(b_last + m_prev, jnp.max(dec, axis=-1, keepdims=True))
        w_s = jnp.exp(dec - m_new)
        w_c = jnp.exp(b_last + m_prev - m_new)
        vw = (vh.T * w_s).astype(BF16)
        c_ref[h] = w_c * c_st + jnp.dot(vw, k16, preferred_element_type=F32)
        w_s8 = jnp.broadcast_to(w_s, (8, L)).astype(BF16)
        n_ref[h:h + 1, :] = w_c * n_st + jnp.dot(w_s8, k16, preferred_element_type=F32)[0:1]
        m_ref[h:h + 1, :] = jnp.broadcast_to(m_new, (1, LANES))


def _mlstm(z, grow, c0, n0, m0, norm_w, *, row_block0, n_chunks, seg_shift, per_batch_seg, heads, dqk, dv):
    nb = c0.shape[0]
    nqk = heads * dqk
    nv = heads * dv
    qk_blocks_before_v = (2 * nqk) // nv
    if per_batch_seg:
        rb = lambda b, c: row_block0
        n_rows = CHUNK
        out_rb = lambda b, c: 0
    else:
        rb = lambda b, c: row_block0 + b * n_chunks + c
        n_rows = nb * n_chunks * CHUNK
        out_rb = lambda b, c: b * n_chunks + c
    body = functools.partial(_mlstm_body, heads=heads, dqk=dqk, dv=dv, seg_shift=seg_shift,
                             per_batch_seg=per_batch_seg)
    return pl.pallas_call(
        body,
        grid=(nb, n_chunks),
        in_specs=[pl.BlockSpec((CHUNK, nqk), lambda b, c: (rb(b, c), 0)),
                  pl.BlockSpec((CHUNK, nqk), lambda b, c: (rb(b, c), 1)),
                  pl.BlockSpec((CHUNK, nv), lambda b, c: (rb(b, c), qk_blocks_before_v)),
                  pl.BlockSpec((CHUNK, nv), lambda b, c: (rb(b, c), qk_blocks_before_v + 1)),
                  pl.BlockSpec((2 * heads, CHUNK), lambda b, c: (0, rb(b, c))),
                  pl.BlockSpec((None, heads, dv, dqk), lambda b, c: (b, 0, 0, 0)),
                  pl.BlockSpec((None, heads, dqk), lambda b, c: (b, 0, 0)),
                  pl.BlockSpec((None, heads, LANES), lambda b, c: (b, 0, 0)),
                  pl.BlockSpec((1, nv), lambda b, c: (0, 0))],
        out_specs=[pl.BlockSpec((CHUNK, nv), lambda b, c: (out_rb(b, c), 0)),
                   pl.BlockSpec((None, heads, dv, dqk), lambda b, c: (b, 0, 0, 0)),
                   pl.BlockSpec((None, heads, dqk), lambda b, c: (b, 0, 0)),
                   pl.BlockSpec((None, heads, LANES), lambda b, c: (b, 0, 0))],
        out_shape=[jax.ShapeDtypeStruct((n_rows, nv), BF16),
                   jax.ShapeDtypeStruct((nb, heads, dv, dqk), F32),
                   jax.ShapeDtypeStruct((nb, heads, dqk), F32),
                   jax.ShapeDtypeStruct((nb, heads, LANES), F32)],
        compiler_params=_cparams("arbitrary", "arbitrary"),
        name="mlstm_sample" if per_batch_seg else "mlstm_prompt",
    )(z, z, z, z, grow, c0, n0, m0, norm_w.reshape(1, nv))


def _cumsum_body(lf_ref, o_ref):
    t = lf_ref.shape[1]
    si = lax.broadcasted_iota(jnp.int32, (LANES, LANES), 0)
    li = lax.broadcasted_iota(jnp.int32, (LANES, LANES), 1)
    incl = jnp.where(si <= li, 1.0, 0.0).astype(BF16)
    carry = jnp.zeros((lf_ref.shape[0], 1), F32)
    for r in range(t // LANES):
        blk = _dot01(lf_ref[:, r * LANES:(r + 1) * LANES], incl) + carry
        o_ref[:, r * LANES:(r + 1) * LANES] = blk
        carry = blk[:, LANES - 1:LANES]


def _cumsum_rows(lf_rows, nb, t):
    hh = lf_rows.shape[0]
    return pl.pallas_call(
        _cumsum_body,
        grid=(nb,),
        in_specs=[pl.BlockSpec((hh, t), lambda b: (0, b))],
        out_specs=pl.BlockSpec((None, hh, t), lambda b: (b, 0, 0)),
        out_shape=jax.ShapeDtypeStruct((nb, hh, t), F32),
        compiler_params=_cparams("parallel"),
        name="fox_cumsum",
    )(lf_rows)


def _fox_attn_body(q_ref, k_ref, v_ref, g_ref, cq_ref, ck_ref, o_ref, cqb_s, *, tq, nq, scale):
    qi = pl.program_id(2)
    for r in range(tq // LANES):
        row = cq_ref[:, r * LANES:(r + 1) * LANES]
        cqb_s[r * LANES:(r + 1) * LANES, :] = jnp.broadcast_to(row, (LANES, LANES)).T
    q16 = q_ref[...].astype(BF16)

    for j in range(nq):
        @pl.when(qi == j)
        def _(j=j):
            n_keys = (j + 1) * tq
            k16 = k_ref[0:n_keys, :].astype(BF16)
            v16 = v_ref[0:n_keys, :].astype(BF16)
            s = lax.dot_general(q16, k16, NT_DIMS, preferred_element_type=F32) * scale
            s = s + jnp.concatenate([cqb_s[...]] * (n_keys // LANES), axis=1) - ck_ref[:, 0:n_keys]
            rr = lax.broadcasted_iota(jnp.int32, (tq, tq), 0)
            cc = lax.broadcasted_iota(jnp.int32, (tq, tq), 1)
            tail = jnp.where(cc <= rr, s[:, j * tq:], -jnp.inf)
            s = tail if j == 0 else jnp.concatenate([s[:, :j * tq], tail], axis=1)
            m = jnp.max(s, axis=-1, keepdims=True)
            p = jnp.exp(s - m)
            den = jnp.sum(p, axis=-1, keepdims=True)
            o = jnp.dot(p.astype(BF16), v16, preferred_element_type=F32) / den
            o_ref[...] = (o * jax.nn.sigmoid(g_ref[...])).astype(o_ref.dtype)


def _fox_attn(z, c3, *, nb, t, heads, dh):
    tq = _pick(t, (512, 256, 128))
    nq = t // tq
    scale = dh ** -0.5
    body = functools.partial(_fox_attn_body, tq=tq, nq=nq, scale=scale)
    return pl.pallas_call(
        body,
        grid=(nb, heads, nq),
        in_specs=[pl.BlockSpec((tq, dh), lambda b, h, qi: (b * nq + qi, h)),
                  pl.BlockSpec((t, dh), lambda b, h, qi: (b, heads + h)),
                  pl.BlockSpec((t, dh), lambda b, h, qi: (b, 2 * heads + h)),
                  pl.BlockSpec((tq, dh), lambda b, h, qi: (b * nq + qi, 3 * heads + h)),
                  pl.BlockSpec((None, 1, tq), lambda b, h, qi: (b * heads + h, 0, qi)),
                  pl.BlockSpec((None, 1, t), lambda b, h, qi: (b * heads + h, 0, 0))],
        out_specs=pl.BlockSpec((tq, dh), lambda b, h, qi: (b * nq + qi, h)),
        out_shape=jax.ShapeDtypeStruct((nb * t, heads * dh), BF16),
        scratch_shapes=[pltpu.VMEM((tq, LANES), F32)],
        compiler_params=_cparams("parallel", "parallel", "arbitrary"),
        name="fox_attn",
    )(z, z, z, z, c3, c3)


def _dot01_left(m01, x):
    hi = x.astype(BF16)
    r1 = x - hi.astype(F32)
    mid = r1.astype(BF16)
    lo = (r1 - mid.astype(F32)).astype(BF16)
    d = lambda a: jnp.dot(m01, a, preferred_element_type=F32)
    return d(hi) + d(mid) + d(lo)


def _logf_suffix_body(lf_ref, r_ref, tot_ref, *, heads, rows):
    n = lf_ref.shape[0]
    pb = n // rows
    x = lf_ref[...]
    la = lax.broadcasted_iota(jnp.int32, (LANES, LANES), 0)
    lb = lax.broadcasted_iota(jnp.int32, (LANES, LANES), 1)
    same_head = (la % heads) == (lb % heads)
    later01 = jnp.where(same_head & (la > lb), 1.0, 0.0).astype(BF16)
    head01 = jnp.where(same_head, 1.0, 0.0).astype(BF16)
    within = _dot01(x, later01)
    row_tot = _dot01(x, head01)
    ra = lax.broadcasted_iota(jnp.int32, (LANES, LANES), 0)
    rb = lax.broadcasted_iota(jnp.int32, (LANES, LANES), 1)
    below01 = jnp.where((ra // rows == rb // rows) & (rb > ra), 1.0, 0.0).astype(BF16)
    for g in range(n // LANES):
        sl = slice(g * LANES, (g + 1) * LANES)
        r_ref[sl, :] = within[sl] + _dot01_left(below01, row_tot[sl])
    pa = lax.broadcasted_iota(jnp.int32, (pb, n), 0)
    pr = lax.broadcasted_iota(jnp.int32, (pb, n), 1)
    page01 = jnp.where(pr // rows == pa, 1.0, 0.0).astype(BF16)
    tot_ref[...] = _dot01_left(page01, row_tot)


def _logf_suffix(lf2, *, heads, rows):
    n_phys = lf2.shape[0] // rows
    pb = _pick(n_phys, (64, 32, 16, 8))
    assert (pb * rows) % LANES == 0 and LANES % rows == 0
    body = functools.partial(_logf_suffix_body, heads=heads, rows=rows)
    return pl.pallas_call(
        body,
        grid=(n_phys // pb,),
        in_specs=[pl.BlockSpec((pb * rows, LANES), lambda i: (i, 0))],
        out_specs=[pl.BlockSpec((pb * rows, LANES), lambda i: (i, 0)),
                   pl.BlockSpec((pb, LANES), lambda i: (i, 0))],
        out_shape=[jax.ShapeDtypeStruct((n_phys * rows, LANES), F32), jax.ShapeDtypeStruct((n_phys, LANES), F32)],
        compiler_params=_cparams("parallel"),
        name="logf_suffix",
    )(lf2)


def _fox_decode_body(pt_ref, q_ref, kn_ref, vn_ref, g_ref, lfn_ref, kp_ref, vp_ref, rl_ref, tot_ref, o_ref,
                     m_s, l_s, acc_s, carry_s, cn_s, mask_s, kn_s, vn_s, *, heads, n_new, n_pages, scale):
    p = pl.program_id(1)
    nq = n_new * heads
    nkeys = kp_ref.shape[0]
    rows_per_page = rl_ref.shape[0]
    q16 = q_ref[...].astype(BF16)

    def update(lg, v16):
        rep = lg.shape[1] // LANES
        m_prev = m_s[...]
        m_new = jnp.maximum(m_prev, jnp.max(lg, axis=-1, keepdims=True))
        pp = jnp.exp(lg - jnp.concatenate([m_new] * rep, axis=1))
        corr = jnp.exp(m_prev - m_new)
        l_s[...] = corr * l_s[...] + jnp.sum(pp, axis=-1, keepdims=True)
        acc_s[...] = corr * acc_s[...] + jnp.dot(pp.astype(BF16), v16, preferred_element_type=F32)
        m_s[...] = m_new

    @pl.when(p == 0)
    def _():
        m_s[...] = jnp.full(m_s.shape, -jnp.inf, F32)
        l_s[...] = jnp.zeros(l_s.shape, F32)
        acc_s[...] = jnp.zeros(acc_s.shape, F32)
        carry_s[...] = jnp.zeros(carry_s.shape, F32)
        r_i = lax.broadcasted_iota(jnp.int32, (nq, nkeys), 0)
        c_i = lax.broadcasted_iota(jnp.int32, (nq, nkeys), 1)
        mask_s[...] = jnp.where((r_i % heads) == (c_i % heads), 0.0, -jnp.inf)
        ra = lax.broadcasted_iota(jnp.int32, (nq, nq), 0)
        rb = lax.broadcasted_iota(jnp.int32, (nq, nq), 1)
        upto01 = jnp.where(((ra % heads) == (rb % heads)) & (rb <= ra), 1.0, 0.0).astype(BF16)
        cn = _dot01_left(upto01, lfn_ref[...])
        cn_s[...] = cn
        kn_s[...] = jnp.zeros(kn_s.shape, F32)
        vn_s[...] = jnp.zeros(vn_s.shape, F32)
        kn_s[0:nq, :] = kn_ref[...]
        vn_s[0:nq, :] = vn_ref[...]
        s = lax.dot_general(q16, kn_s[...].astype(BF16), NT_DIMS, preferred_element_type=F32)
        cn_row = _rows_to_cols(cn)[0:1, :]
        rn = lax.broadcasted_iota(jnp.int32, (nq, LANES), 0)
        cc = lax.broadcasted_iota(jnp.int32, (nq, LANES), 1)
        ok = ((rn % heads) == (cc % heads)) & (cc <= rn) & (cc < nq)
        lg = jnp.where(ok, s * scale + cn - cn_row, -jnp.inf)
        update(lg, vn_s[...].astype(BF16))

    s = lax.dot_general(q16, kp_ref[...].astype(BF16), NT_DIMS, preferred_element_type=F32)
    bias = rl_ref[...] + carry_s[0:1, :]
    bias_full = jnp.concatenate([jnp.broadcast_to(bias[i:i + 1, :], (nq, LANES)) for i in range(rows_per_page)],
                                axis=1)
    cn_full = jnp.concatenate([cn_s[...]] * rows_per_page, axis=1)
    lg = s * scale + cn_full + bias_full + mask_s[...]
    update(lg, vp_ref[...].astype(BF16))
    carry_s[...] = carry_s[...] + tot_ref[...]

    @pl.when(p == n_pages - 1)
    def _():
        o = acc_s[...] / l_s[...]
        o_ref[...] = (o * jax.nn.sigmoid(g_ref[...])).astype(o_ref.dtype)


def _fox_decode(q_r, kn_r, vn_r, g_r, lfn_r, cache_k2, cache_v2, r_local, tot, page_table, *, heads, n_new):
    nb, n_pages = page_table.shape
    nq, dh = q_r.shape[1:]
    nkeys = cache_k2.shape[1]
    rows = r_local.shape[1]
    assert dh == LANES and nq <= LANES and nq % 8 == 0
    scale = dh ** -0.5
    body = functools.partial(_fox_decode_body, heads=heads, n_new=n_new, n_pages=n_pages, scale=scale)
    page = lambda b, p, pt: pt[b, n_pages - 1 - p]
    new_spec = pl.BlockSpec((None, nq, dh), lambda b, p, pt: (b, 0, 0))
    grid_spec = pltpu.PrefetchScalarGridSpec(
        num_scalar_prefetch=1,
        grid=(nb, n_pages),
        in_specs=[new_spec, new_spec, new_spec, new_spec, new_spec,
                  pl.BlockSpec((None, nkeys, dh), lambda b, p, pt: (page(b, p, pt), 0, 0)),
                  pl.BlockSpec((None, nkeys, dh), lambda b, p, pt: (page(b, p, pt), 0, 0)),
                  pl.BlockSpec((None, rows, LANES), lambda b, p, pt: (page(b, p, pt), 0, 0)),
                  pl.BlockSpec((None, 1, LANES), lambda b, p, pt: (page(b, p, pt), 0, 0))],
        out_specs=pl.BlockSpec((None, nq, dh), lambda b, p, pt: (b, 0, 0)),
        scratch_shapes=[pltpu.VMEM((nq, LANES), F32), pltpu.VMEM((nq, LANES), F32), pltpu.VMEM((nq, dh), F32),
                        pltpu.VMEM((8, LANES), F32), pltpu.VMEM((nq, LANES), F32), pltpu.VMEM((nq, nkeys), F32),
                        pltpu.VMEM((LANES, dh), F32), pltpu.VMEM((LANES, dh), F32)],
    )
    return pl.pallas_call(
        body,
        grid_spec=grid_spec,
        out_shape=jax.ShapeDtypeStruct((nb, nq, dh), BF16),
        compiler_params=_cparams("parallel", "arbitrary"),
        name="fox_decode",
    )(page_table, q_r, kn_r, vn_r, g_r, lfn_r, cache_k2, cache_v2, r_local, tot)


def _ffn_weights(w_in, w_out):
    d, f2 = w_in.shape
    f = f2 // 2
    fp = -(-f // FF_ALIGN) * FF_ALIGN
    wg = jnp.pad(w_in[:, :f].astype(BF16), ((0, 0), (0, fp - f)))
    wu = jnp.pad(w_in[:, f:].astype(BF16), ((0, 0), (0, fp - f)))
    wo = jnp.pad(w_out.astype(BF16), ((0, fp - f), (0, 0)))
    return wg, wu, wo


def kernel(x_prompt, x_sample, state_C, state_n, state_m, cache_k, cache_v, cache_logf, page_table,
           ln_g, ln_b, ffn_w_in, ffn_w_out, m_w_in, m_b_i, m_b_f, m_norm, m_w_out,
           f_w_in, f_b_f, f_q_norm, f_k_norm, f_w_out):
    nb_p, t, d = x_prompt.shape
    nb_s, t_s, _ = x_sample.shape
    depth = ln_g.shape[0]
    alpha = (2 * depth) ** 0.25
    n_p = nb_p * t
    n_s = nb_s * t_s
    assert n_s == CHUNK and t % CHUNK == 0 and (t_s & (t_s - 1)) == 0
    m_heads = m_b_i.shape[0]
    m_dqk = d // (2 * m_heads)
    m_dv = d // m_heads
    f_heads = f_b_f.shape[0]
    f_dh = d // f_heads
    nqk = m_heads * m_dqk
    nv = m_heads * m_dv

    x32 = jnp.concatenate([x_prompt.reshape(n_p, d), x_sample.reshape(n_s, d)], axis=0)
    x16 = x32.astype(BF16)

    def ffn(x32, x16, layer, which, g, b):
        wg, wu, wo = _ffn_weights(ffn_w_in[layer, which], ffn_w_out[layer, which])
        hid = _ffn_in(x16, wg, wu)
        return _mm_res_ln(hid, wo, x32, g, b, alpha=alpha, scale=0.5)

    outs = {}
    for layer in range(depth):
        g, b = ln_g[layer], ln_b[layer]
        x32, x16 = ffn(x32, x16, layer, 0, g[0], b[0])

        if layer % 2 == 0:
            n_main = 2 * nqk + 2 * nv
            colscale = jnp.concatenate([jnp.ones((nqk,), F32), jnp.full((nqk,), m_dqk ** -0.5, F32),
                                        jnp.ones((2 * nv,), F32)])
            z = _proj(x16, m_w_in[:, :n_main].astype(BF16), colscale)
            _, grow = _gates(x16, m_w_in[:, n_main:], jnp.concatenate([m_b_i, m_b_f]), ls_from=m_heads)
            common = dict(heads=m_heads, dqk=m_dqk, dv=m_dv)
            zero_state = (jnp.zeros((nb_p, m_heads, m_dv, m_dqk), F32), jnp.zeros((nb_p, m_heads, m_dqk), F32),
                          jnp.zeros((nb_p, m_heads, LANES), F32))
            hp, c_p, n_pr, m_p = _mlstm(z, grow, *zero_state, m_norm, row_block0=0, n_chunks=t // CHUNK,
                                        seg_shift=int(math.log2(CHUNK)), per_batch_seg=False, **common)
            m0_s = jnp.broadcast_to(state_m.astype(F32)[:, :, None], (nb_s, m_heads, LANES))
            hs, c_s, n_sm, m_s = _mlstm(z, grow, state_C.astype(F32), state_n.astype(F32), m0_s, m_norm,
                                        row_block0=n_p // CHUNK, n_chunks=1, seg_shift=int(math.log2(t_s)),
                                        per_batch_seg=True, **common)
            y16 = jnp.concatenate([hp, hs], axis=0)
            w_mix = m_w_out
            outs.update(c_p=c_p, n_p=n_pr, m_p=m_p[:, :, 0], c_s=c_s, n_s=n_sm, m_s=m_s[:, :, 0])
        else:
            n_main = 4 * d
            colscale = jnp.concatenate([jnp.tile(f_q_norm.astype(F32), f_heads), jnp.tile(f_k_norm.astype(F32), f_heads),
                                        jnp.ones((2 * d,), F32)])
            z = _proj(x16, f_w_in[:, :n_main].astype(BF16), colscale, norm_cols=2 * d, group=f_dh)
            lf_cols, lf_rows = _gates(x16, f_w_in[:, n_main:], f_b_f, ls_from=0)
            c3 = _cumsum_rows(lf_rows, nb_p, t).reshape(nb_p * f_heads, 1, t)
            yp = _fox_attn(z, c3, nb=nb_p, t=t, heads=f_heads, dh=f_dh)
            n_phys, page_len = cache_k.shape[:2]
            rows = page_len * f_heads // LANES
            r_local, tot = _logf_suffix(cache_logf.astype(F32).reshape(n_phys * rows, LANES), heads=f_heads, rows=rows)
            zs = z[n_p:]
            per_row = lambda a: a.reshape(nb_s, t_s * f_heads, f_dh)
            lfn_r = jnp.broadcast_to(lf_cols[n_p:, :f_heads].reshape(nb_s, t_s * f_heads, 1),
                                     (nb_s, t_s * f_heads, LANES))
            ys = _fox_decode(per_row(zs[:, :d]), per_row(zs[:, d:2 * d]), per_row(zs[:, 2 * d:3 * d]),
                             per_row(zs[:, 3 * d:]), lfn_r,
                             cache_k.reshape(n_phys, page_len * f_heads, f_dh),
                             cache_v.reshape(n_phys, page_len * f_heads, f_dh),
                             r_local.reshape(n_phys, rows, LANES), tot.reshape(n_phys, 1, LANES), page_table,
                             heads=f_heads, n_new=t_s)
            y16 = jnp.concatenate([yp, ys.reshape(n_s, d)], axis=0)
            w_mix = f_w_out
            kk = z[:, d:2 * d]
            vv = z[:, 2 * d:3 * d]
            lf = lf_cols[:, :f_heads]
            outs.update(k_p=kk[:n_p].reshape(nb_p, t, f_heads, f_dh), v_p=vv[:n_p].reshape(nb_p, t, f_heads, f_dh),
                        lf_p=lf[:n_p].reshape(nb_p, t, f_heads),
                        k_s=kk[n_p:].reshape(nb_s, t_s, f_heads, f_dh), v_s=vv[n_p:].reshape(nb_s, t_s, f_heads, f_dh),
                        lf_s=lf[n_p:].reshape(nb_s, t_s, f_heads))

        x32, x16 = _mm_res_ln(y16, w_mix.astype(BF16), x32, g[1], b[1], alpha=alpha, scale=1.0)
        x32, x16 = ffn(x32, x16, layer, 1, g[2], b[2])

    xp = x32[:n_p].reshape(nb_p, t, d)
    xs = x32[n_p:].reshape(nb_s, t_s, d)
    return (xp, xs, outs["c_p"], outs["n_p"], outs["m_p"], outs["c_s"], outs["n_s"], outs["m_s"],
            outs["k_p"], outs["v_p"], outs["lf_p"], outs["k_s"], outs["v_s"], outs["lf_s"])
```

```python
import functools
import math

import jax
import jax.numpy as jnp
from jax import lax
from jax.experimental import pallas as pl
from jax.experimental.pallas import tpu as pltpu

F32 = jnp.float32
BF16 = jnp.bfloat16
LN_EPS = 1e-5
NORM_EPS = 1e-6
LANES = 128
CHUNK = 128
PAGE = 128
FF_ALIGN = 512
DECODE_PAGES_PER_STEP = 4
VMEM_LIMIT = 56 * 1024 * 1024
NT_DIMS = (((1,), (1,)), ((), ()))


def _cparams(*sem):
    return pltpu.CompilerParams(dimension_semantics=sem, vmem_limit_bytes=VMEM_LIMIT)


def _pick(n, cands):
    for c in cands:
        if c <= n and n % c == 0:
            return c
    return n


def _log_sigmoid(x):
    return jnp.minimum(x, 0.0) - jnp.log1p(jnp.exp(-jnp.abs(x)))


def _dot01(x, m01):
    hi = x.astype(BF16)
    r1 = x - hi.astype(F32)
    mid = r1.astype(BF16)
    lo = (r1 - mid.astype(F32)).astype(BF16)
    d = lambda a: jnp.dot(a, m01, preferred_element_type=F32)
    return d(hi) + d(mid) + d(lo)


def _rows_to_cols(rows):
    r = rows.shape[0]
    if r < LANES:
        rows = jnp.concatenate([rows, jnp.zeros((LANES - r, LANES), F32)], axis=0)
    return rows.T


def _ffn_in_body(x_ref, wg_ref, wu_ref, h_ref):
    x = x_ref[...]
    g = jnp.dot(x, wg_ref[...], preferred_element_type=F32)
    u = jnp.dot(x, wu_ref[...], preferred_element_type=F32)
    h_ref[...] = (g * jax.nn.sigmoid(g) * u).astype(h_ref.dtype)


def _ffn_in(x16, wg, wu):
    n, d = x16.shape
    f = wg.shape[1]
    tm = _pick(n, (832, 640, 528, 512, 256, 128))
    tf = _pick(f, (512, 256, 128))
    return pl.pallas_call(
        _ffn_in_body,
        grid=(n // tm, f // tf),
        in_specs=[pl.BlockSpec((tm, d), lambda i, j: (i, 0)),
                  pl.BlockSpec((d, tf), lambda i, j: (0, j)),
                  pl.BlockSpec((d, tf), lambda i, j: (0, j))],
        out_specs=pl.BlockSpec((tm, tf), lambda i, j: (i, j)),
        out_shape=jax.ShapeDtypeStruct((n, f), BF16),
        compiler_params=_cparams("parallel", "arbitrary"),
        name="ffn_in",
    )(x16, wg, wu)


def _mm_res_ln_body(a_ref, w_ref, res_ref, g_ref, b_ref, o32_ref, o16_ref, *, alpha, scale):
    acc = jnp.dot(a_ref[...], w_ref[...], preferred_element_type=F32)
    z = alpha * res_ref[...] + scale * acc
    mu = jnp.mean(z, axis=-1, keepdims=True)
    zc = z - mu
    var = jnp.mean(zc * zc, axis=-1, keepdims=True)
    y = zc * lax.rsqrt(var + LN_EPS) * g_ref[...] + b_ref[...]
    o32_ref[...] = y
    o16_ref[...] = y.astype(BF16)


def _mm_res_ln(a16, w16, res32, g, b, *, alpha, scale):
    n, kdim = a16.shape
    d = w16.shape[1]
    tm = _pick(n, (320, 256, 128))
    body = functools.partial(_mm_res_ln_body, alpha=alpha, scale=scale)
    return pl.pallas_call(
        body,
        grid=(n // tm,),
        in_specs=[pl.BlockSpec((tm, kdim), lambda i: (i, 0)),
                  pl.BlockSpec((kdim, d), lambda i: (0, 0), pipeline_mode=pl.Buffered(1)),
                  pl.BlockSpec((tm, d), lambda i: (i, 0)),
                  pl.BlockSpec((1, d), lambda i: (0, 0)),
                  pl.BlockSpec((1, d), lambda i: (0, 0))],
        out_specs=[pl.BlockSpec((tm, d), lambda i: (i, 0)),
                   pl.BlockSpec((tm, d), lambda i: (i, 0))],
        out_shape=[jax.ShapeDtypeStruct((n, d), F32), jax.ShapeDtypeStruct((n, d), BF16)],
        compiler_params=_cparams("parallel"),
        name="mm_res_ln",
    )(a16, w16, res32, g.reshape(1, d), b.reshape(1, d))


def _proj_body(x_ref, w_ref, cs_ref, o_ref, *, n_norm_tiles, group):
    z = jnp.dot(x_ref[...], w_ref[...], preferred_element_type=F32)
    cs = cs_ref[...]
    if n_norm_tiles == 0:
        o_ref[...] = z * cs
        return
    j = pl.program_id(1)

    @pl.when(j >= n_norm_tiles)
    def _():
        o_ref[...] = z * cs

    @pl.when(j < n_norm_tiles)
    def _():
        for s in range(z.shape[1] // group):
            sl = slice(s * group, (s + 1) * group)
            zh = z[:, sl]
            ms = jnp.mean(zh * zh, axis=-1, keepdims=True)
            o_ref[:, sl] = zh * lax.rsqrt(ms + NORM_EPS) * cs[:, sl]


def _proj(x16, w16, colscale, *, norm_cols=0, group=LANES):
    n, d = x16.shape
    e = w16.shape[1]
    tm = _pick(n, (832, 640, 528, 512, 256, 128))
    tn = _pick(e, (512, 256, 128))
    body = functools.partial(_proj_body, n_norm_tiles=norm_cols // tn, group=group)
    return pl.pallas_call(
        body,
        grid=(n // tm, e // tn),
        in_specs=[pl.BlockSpec((tm, d), lambda i, j: (i, 0)),
                  pl.BlockSpec((d, tn), lambda i, j: (0, j)),
                  pl.BlockSpec((1, tn), lambda i, j: (0, j))],
        out_specs=pl.BlockSpec((tm, tn), lambda i, j: (i, j)),
        out_shape=jax.ShapeDtypeStruct((n, e), F32),
        compiler_params=_cparams("parallel", "arbitrary"),
        name="proj",
    )(x16, w16, colscale.reshape(1, e))


def _gate_body(x_ref, wc_ref, bc_ref, oc_ref, or_ref, *, ls_from, n_gates):
    zc = jnp.dot(x_ref[...], wc_ref[...], preferred_element_type=F32) + bc_ref[...]
    lane = lax.broadcasted_iota(jnp.int32, zc.shape, 1)
    act = jnp.where((lane >= ls_from) & (lane < n_gates), _log_sigmoid(zc), zc)
    oc_ref[...] = act
    for r in range(zc.shape[0] // LANES):
        or_ref[:, r * LANES:(r + 1) * LANES] = act[r * LANES:(r + 1) * LANES, :].T[:n_gates]


def _gates(x16, w_gate, bias, *, ls_from):
    n, d = x16.shape
    ng = w_gate.shape[1]
    assert ng % 8 == 0 and ng <= LANES
    tm = _pick(n, (1664, 1280, 1024, 640, 512, 256, 128))
    assert tm % LANES == 0
    wc = jnp.pad(w_gate, ((0, 0), (0, LANES - ng))).astype(BF16)
    bc = jnp.pad(bias.astype(F32), (0, LANES - ng)).reshape(1, LANES)
    body = functools.partial(_gate_body, ls_from=ls_from, n_gates=ng)
    return pl.pallas_call(
        body,
        grid=(n // tm,),
        in_specs=[pl.BlockSpec((tm, d), lambda i: (i, 0)),
                  pl.BlockSpec((d, LANES), lambda i: (0, 0)),
                  pl.BlockSpec((1, LANES), lambda i: (0, 0))],
        out_specs=[pl.BlockSpec((tm, LANES), lambda i: (i, 0)),
                   pl.BlockSpec((ng, tm), lambda i: (0, i))],
        out_shape=[jax.ShapeDtypeStruct((n, LANES), F32), jax.ShapeDtypeStruct((ng, n), F32)],
        compiler_params=_cparams("parallel"),
        name="gates",
    )(x16, wc, bc)


def _mlstm_body(q_ref, k_ref, v_ref, o_ref, g_ref, c0_ref, n0_ref, m0_ref, nw_ref,
                h_ref, c_ref, n_ref, m_ref, *, heads, dqk, dv, seg_shift, per_batch_seg):
    b = pl.program_id(0)
    c = pl.program_id(1)
    L = CHUNK
    seg_len = 1 << seg_shift
    seg = b if per_batch_seg else 0

    @pl.when(c == 0)
    def _():
        c_ref[...] = c0_ref[...]
        n_ref[...] = n0_ref[...]
        m_ref[...] = m0_ref[...]

    if per_batch_seg:
        @pl.when(b == 0)
        def _():
            h_ref[...] = jnp.zeros(h_ref.shape, h_ref.dtype)

    li = lax.broadcasted_iota(jnp.int32, (1, L), 1)
    si = lax.broadcasted_iota(jnp.int32, (L, 1), 0)
    lseg = lax.shift_right_logical(li, seg_shift)
    sseg = lax.shift_right_logical(si, seg_shift)
    pair_ok = (lseg == sseg) & (li <= si)
    row_live = sseg == seg
    lane_live = lseg == seg
    last_lane = seg * seg_len + (seg_len - 1)
    cum01 = jnp.where((lax.shift_right_logical(si, seg_shift) == lseg) & (si <= li), 1.0, 0.0).astype(BF16)

    gates = g_ref[...]
    ig = gates[:heads]
    lf = gates[heads:]
    bcum = _dot01(lf, cum01)
    src = ig - bcum
    bcols = _rows_to_cols(bcum)

    for h in range(heads):
        b_col = bcols[:, h:h + 1]
        b_row = bcum[h:h + 1, :]
        m_prev = m_ref[h:h + 1, 0:1]
        dmat = jnp.where(pair_ok, b_col + src[h:h + 1, :], -jnp.inf)
        inter = b_col + m_prev
        m_t = jnp.maximum(jnp.max(dmat, axis=-1, keepdims=True), inter)
        w_intra = jnp.exp(dmat - m_t)
        w_inter = jnp.exp(inter - m_t)

        qh = q_ref[:, h * dqk:(h + 1) * dqk]
        q16 = qh.astype(BF16)
        k16 = k_ref[:, h * dqk:(h + 1) * dqk].astype(BF16)
        vh = v_ref[:, h * dv:(h + 1) * dv]
        v16 = vh.astype(BF16)
        c_st = c_ref[h]
        n_st = n_ref[h:h + 1, :]

        s = lax.dot_general(q16, k16, NT_DIMS, preferred_element_type=F32) * w_intra
        num = (w_inter * lax.dot_general(q16, c_st.astype(BF16), NT_DIMS, preferred_element_type=F32)
               + jnp.dot(s.astype(BF16), v16, preferred_element_type=F32))
        den = (w_inter * jnp.sum(qh * n_st, axis=-1, keepdims=True)
               + jnp.sum(s, axis=-1, keepdims=True))
        hv = num / jnp.maximum(jnp.abs(den), jnp.exp(-m_t))
        ms = jnp.mean(hv * hv, axis=-1, keepdims=True)
        hn = hv * lax.rsqrt(ms + NORM_EPS) * nw_ref[:, h * dv:(h + 1) * dv]
        out = (hn * jax.nn.sigmoid(o_ref[:, h * dv:(h + 1) * dv])).astype(h_ref.dtype)
        if per_batch_seg:
            out = jnp.where(row_live, out, h_ref[:, h * dv:(h + 1) * dv])
        h_ref[:, h * dv:(h + 1) * dv] = out

        b_last = jnp.sum(jnp.where(li == last_lane, b_row, 0.0), axis=-1, keepdims=True)
        dec = jnp.where(lane_live, b_last - b_row + ig[h:h + 1, :], -jnp.inf)
        m_new = jnp.maximum(b_last + m_prev, jnp.max(dec, axis=-1, keepdims=True))
        w_s = jnp.exp(dec - m_new)
        w_c = jnp.exp(b_last + m_prev - m_new)
        vw = (vh.T * w_s).astype(BF16)
        c_ref[h] = w_c * c_st + jnp.dot(vw, k16, preferred_element_type=F32)
        w_s8 = jnp.broadcast_to(w_s, (8, L)).astype(BF16)
        n_ref[h:h + 1, :] = w_c * n_st + jnp.dot(w_s8, k16, preferred_element_type=F32)[0:1]
        m_ref[h:h + 1, :] = jnp.broadcast_to(m_new, (1, LANES))


def _mlstm(z, grow, c0, n0, m0, norm_w, *, row_block0, n_chunks, seg_shift, per_batch_seg, heads, dqk, dv):
    nb = c0.shape[0]
    nqk = heads * dqk
    nv = heads * dv
    qk_blocks_before_v = (2 * nqk) // nv
    if per_batch_seg:
        rb = lambda b, c: row_block0
        n_rows = CHUNK
        out_rb = lambda b, c: 0
    else:
        rb = lambda b, c: row_block0 + b * n_chunks + c
        n_rows = nb * n_chunks * CHUNK
        out_rb = lambda b, c: b * n_chunks + c
    body = functools.partial(_mlstm_body, heads=heads, dqk=dqk, dv=dv, seg_shift=seg_shift,
                             per_batch_seg=per_batch_seg)
    return pl.pallas_call(
        body,
        grid=(nb, n_chunks),
        in_specs=[pl.BlockSpec((CHUNK, nqk), lambda b, c: (rb(b, c), 0)),
                  pl.BlockSpec((CHUNK, nqk), lambda b, c: (rb(b, c), 1)),
                  pl.BlockSpec((CHUNK, nv), lambda b, c: (rb(b, c), qk_blocks_before_v)),
                  pl.BlockSpec((CHUNK, nv), lambda b, c: (rb(b, c), qk_blocks_before_v + 1)),
                  pl.BlockSpec((2 * heads, CHUNK), lambda b, c: (0, rb(b, c))),
                  pl.BlockSpec((None, heads, dv, dqk), lambda b, c: (b, 0, 0, 0)),
                  pl.BlockSpec((None, heads, dqk), lambda b, c: (b, 0, 0)),
                  pl.BlockSpec((None, heads, LANES), lambda b, c: (b, 0, 0)),
                  pl.BlockSpec((1, nv), lambda b, c: (0, 0))],
        out_specs=[pl.BlockSpec((CHUNK, nv), lambda b, c: (out_rb(b, c), 0)),
                   pl.BlockSpec((None, heads, dv, dqk), lambda b, c: (b, 0, 0, 0)),
                   pl.BlockSpec((None, heads, dqk), lambda b, c: (b, 0, 0)),
                   pl.BlockSpec((None, heads, LANES), lambda b, c: (b, 0, 0))],
        out_shape=[jax.ShapeDtypeStruct((n_rows, nv), BF16),
                   jax.ShapeDtypeStruct((nb, heads, dv, dqk), F32),
                   jax.ShapeDtypeStruct((nb, heads, dqk), F32),
                   jax.ShapeDtypeStruct((nb, heads, LANES), F32)],
        compiler_params=_cparams("arbitrary", "arbitrary"),
        name="mlstm_sample" if per_batch_seg else "mlstm_prompt",
    )(z, z, z, z, grow, c0, n0, m0, norm_w.reshape(1, nv))


def _cumsum_body(lf_ref, o_ref):
    t = lf_ref.shape[1]
    si = lax.broadcasted_iota(jnp.int32, (LANES, LANES), 0)
    li = lax.broadcasted_iota(jnp.int32, (LANES, LANES), 1)
    incl = jnp.where(si <= li, 1.0, 0.0).astype(BF16)
    carry = jnp.zeros((lf_ref.shape[0], 1), F32)
    for r in range(t // LANES):
        blk = _dot01(lf_ref[:, r * LANES:(r + 1) * LANES], incl) + carry
        o_ref[:, r * LANES:(r + 1) * LANES] = blk
        carry = blk[:, LANES - 1:LANES]


def _cumsum_rows(lf_rows, nb, t):
    hh = lf_rows.shape[0]
    return pl.pallas_call(
        _cumsum_body,
        grid=(nb,),
        in_specs=[pl.BlockSpec((hh, t), lambda b: (0, b))],
        out_specs=pl.BlockSpec((None, hh, t), lambda b: (b, 0, 0)),
        out_shape=jax.ShapeDtypeStruct((nb, hh, t), F32),
        compiler_params=_cparams("parallel"),
        name="fox_cumsum",
    )(lf_rows)


def _fox_attn_body(q_ref, k_ref, v_ref, g_ref, cq_ref, ck_ref, o_ref, cqb_s, *, tq, nq, scale):
    qi = pl.program_id(2)
    for r in range(tq // LANES):
        row = cq_ref[:, r * LANES:(r + 1) * LANES]
        cqb_s[r * LANES:(r + 1) * LANES, :] = jnp.broadcast_to(row, (LANES, LANES)).T
    q16 = q_ref[...].astype(BF16)

    for j in range(nq):
        @pl.when(qi == j)
        def _(j=j):
            n_keys = (j + 1) * tq
            k16 = k_ref[0:n_keys, :].astype(BF16)
            v16 = v_ref[0:n_keys, :].astype(BF16)
            s = lax.dot_general(q16, k16, NT_DIMS, preferred_element_type=F32) * scale
            s = s + jnp.concatenate([cqb_s[...]] * (n_keys // LANES), axis=1) - ck_ref[:, 0:n_keys]
            rr = lax.broadcasted_iota(jnp.int32, (tq, tq), 0)
            cc = lax.broadcasted_iota(jnp.int32, (tq, tq), 1)
            tail = jnp.where(cc <= rr, s[:, j * tq:], -jnp.inf)
            s = tail if j == 0 else jnp.concatenate([s[:, :j * tq], tail], axis=1)
            m = jnp.max(s, axis=-1, keepdims=True)
            p = jnp.exp(s - m)
            den = jnp.sum(p, axis=-1, keepdims=True)
            o = jnp.dot(p.astype(BF16), v16, preferred_element_type=F32) / den
            o_ref[...] = (o * jax.nn.sigmoid(g_ref[...])).astype(o_ref.dtype)


def _fox_attn(z, c3, *, nb, t, heads, dh):
    tq = _pick(t, (512, 256, 128))
    nq = t // tq
    scale = dh ** -0.5
    body = functools.partial(_fox_attn_body, tq=tq, nq=nq, scale=scale)
    return pl.pallas_call(
        body,
        grid=(nb, heads, nq),
        in_specs=[pl.BlockSpec((tq, dh), lambda b, h, qi: (b * nq + qi, h)),
                  pl.BlockSpec((t, dh), lambda b, h, qi: (b, heads + h)),
                  pl.BlockSpec((t, dh), lambda b, h, qi: (b, 2 * heads + h)),
                  pl.BlockSpec((tq, dh), lambda b, h, qi: (b * nq + qi, 3 * heads + h)),
                  pl.BlockSpec((None, 1, tq), lambda b, h, qi: (b * heads + h, 0, qi)),
                  pl.BlockSpec((None, 1, t), lambda b, h, qi: (b * heads + h, 0, 0))],
        out_specs=pl.BlockSpec((tq, dh), lambda b, h, qi: (b * nq + qi, h)),
        out_shape=jax.ShapeDtypeStruct((nb * t, heads * dh), BF16),
        scratch_shapes=[pltpu.VMEM((tq, LANES), F32)],
        compiler_params=_cparams("parallel", "parallel", "arbitrary"),
        name="fox_attn",
    )(z, z, z, z, c3, c3)


def _dot01_left(m01, x):
    hi = x.astype(BF16)
    r1 = x - hi.astype(F32)
    mid = r1.astype(BF16)
    lo = (r1 - mid.astype(F32)).astype(BF16)
    d = lambda a: jnp.dot(m01, a, preferred_element_type=F32)
    return d(hi) + d(mid) + d(lo)


def _logf_suffix_body(lf_ref, r_ref, tot_ref, *, heads, rows):
    n = lf_ref.shape[0]
    pb = n // rows
    x = lf_ref[...]
    la = lax.broadcasted_iota(jnp.int32, (LANES, LANES), 0)
    lb = lax.broadcasted_iota(jnp.int32, (LANES, LANES), 1)
    same_head = (la % heads) == (lb % heads)
    later01 = jnp.where(same_head & (la > lb), 1.0, 0.0).astype(BF16)
    head01 = jnp.where(same_head, 1.0, 0.0).astype(BF16)
    within = _dot01(x, later01)
    row_tot = _dot01(x, head01)
    ra = lax.broadcasted_iota(jnp.int32, (LANES, LANES), 0)
    rb = lax.broadcasted_iota(jnp.int32, (LANES, LANES), 1)
    below01 = jnp.where((ra // rows == rb // rows) & (rb > ra), 1.0, 0.0).astype(BF16)
    for g in range(n // LANES):
        sl = slice(g * LANES, (g + 1) * LANES)
        r_ref[sl, :] = within[sl] + _dot01_left(below01, row_tot[sl])
    pa = lax.broadcasted_iota(jnp.int32, (pb, n), 0)
    pr = lax.broadcasted_iota(jnp.int32, (pb, n), 1)
    page01 = jnp.where(pr // rows == pa, 1.0, 0.0).astype(BF16)
    tot_ref[...] = _dot01_left(page01, row_tot)


def _logf_suffix(lf2, *, heads, rows):
    n_phys = lf2.shape[0] // rows
    pb = _pick(n_phys, (64, 32, 16, 8))
    assert (pb * rows) % LANES == 0 and LANES % rows == 0
    body = functools.partial(_logf_suffix_body, heads=heads, rows=rows)
    return pl.pallas_call(
        body,
        grid=(n_phys // pb,),
        in_specs=[pl.BlockSpec((pb * rows, LANES), lambda i: (i, 0))],
        out_specs=[pl.BlockSpec((pb * rows, LANES), lambda i: (i, 0)),
                   pl.BlockSpec((pb, LANES), lambda i: (i, 0))],
        out_shape=[jax.ShapeDtypeStruct((n_phys * rows, LANES), F32), jax.ShapeDtypeStruct((n_phys, LANES), F32)],
        compiler_params=_cparams("parallel"),
        name="logf_suffix",
    )(lf2)


def _fox_decode_body(pt_ref, q_ref, kn_ref, vn_ref, g_ref, lfn_ref, *rest, heads, n_new, n_steps, pages_per_step, scale):
    page_refs = rest[:4 * pages_per_step]
    o_ref, m_s, l_s, acc_s, carry_s, cn_s, mask_s, kn_s, vn_s = rest[4 * pages_per_step:]
    p = pl.program_id(1)
    nq = n_new * heads
    nkeys = page_refs[0].shape[0]
    rows_per_page = page_refs[2].shape[0]
    q16 = q_ref[...].astype(BF16)

    def update(lg, v16s):
        rep = lg.shape[1] // LANES
        width = lg.shape[1] // len(v16s)
        m_prev = m_s[...]
        m_new = jnp.maximum(m_prev, jnp.max(lg, axis=-1, keepdims=True))
        pp = jnp.exp(lg - jnp.concatenate([m_new] * rep, axis=1))
        corr = jnp.exp(m_prev - m_new)
        l_s[...] = corr * l_s[...] + jnp.sum(pp, axis=-1, keepdims=True)
        pv = jnp.dot(pp[:, :width].astype(BF16), v16s[0], preferred_element_type=F32)
        for i in range(1, len(v16s)):
            pv = pv + jnp.dot(pp[:, i * width:(i + 1) * width].astype(BF16), v16s[i], preferred_element_type=F32)
        acc_s[...] = corr * acc_s[...] + pv
        m_s[...] = m_new

    @pl.when(p == 0)
    def _():
        m_s[...] = jnp.full(m_s.shape, -jnp.inf, F32)
        l_s[...] = jnp.zeros(l_s.shape, F32)
        acc_s[...] = jnp.zeros(acc_s.shape, F32)
        carry_s[...] = jnp.zeros(carry_s.shape, F32)
        r_i = lax.broadcasted_iota(jnp.int32, (nq, nkeys), 0)
        c_i = lax.broadcasted_iota(jnp.int32, (nq, nkeys), 1)
        mask_s[...] = jnp.where((r_i % heads) == (c_i % heads), 0.0, -jnp.inf)
        ra = lax.broadcasted_iota(jnp.int32, (nq, nq), 0)
        rb = lax.broadcasted_iota(jnp.int32, (nq, nq), 1)
        upto01 = jnp.where(((ra % heads) == (rb % heads)) & (rb <= ra), 1.0, 0.0).astype(BF16)
        cn = _dot01_left(upto01, lfn_ref[...])
        cn_s[...] = cn
        kn_s[...] = jnp.zeros(kn_s.shape, F32)
        vn_s[...] = jnp.zeros(vn_s.shape, F32)
        kn_s[0:nq, :] = kn_ref[...]
        vn_s[0:nq, :] = vn_ref[...]
        s = lax.dot_general(q16, kn_s[...].astype(BF16), NT_DIMS, preferred_element_type=F32)
        cn_row = _rows_to_cols(cn)[0:1, :]
        rn = lax.broadcasted_iota(jnp.int32, (nq, LANES), 0)
        cc = lax.broadcasted_iota(jnp.int32, (nq, LANES), 1)
        ok = ((rn % heads) == (cc % heads)) & (cc <= rn) & (cc < nq)
        lg = jnp.where(ok, s * scale + cn - cn_row, -jnp.inf)
        update(lg, [vn_s[...].astype(BF16)])

    cn_full = jnp.concatenate([cn_s[...]] * rows_per_page, axis=1)
    carry = carry_s[0:1, :]
    lgs, v16s = [], []
    for i in range(pages_per_step):
        kp_ref, vp_ref, rl_ref, tot_ref = page_refs[4 * i:4 * i + 4]
        s = lax.dot_general(q16, kp_ref[...].astype(BF16), NT_DIMS, preferred_element_type=F32)
        bias = rl_ref[...] + carry
        bias_full = jnp.concatenate([jnp.broadcast_to(bias[j:j + 1, :], (nq, LANES)) for j in range(rows_per_page)],
                                    axis=1)
        lgs.append(s * scale + cn_full + bias_full + mask_s[...])
        v16s.append(vp_ref[...].astype(BF16))
        carry = carry + tot_ref[...]
    update(lgs[0] if pages_per_step == 1 else jnp.concatenate(lgs, axis=1), v16s)
    carry_s[...] = jnp.broadcast_to(carry, carry_s.shape)

    @pl.when(p == n_steps - 1)
    def _():
        o = acc_s[...] / l_s[...]
        o_ref[...] = (o * jax.nn.sigmoid(g_ref[...])).astype(o_ref.dtype)


def _fox_decode(q_r, kn_r, vn_r, g_r, lfn_r, cache_k2, cache_v2, r_local, tot, page_table, *, heads, n_new):
    nb, n_pages = page_table.shape
    nq, dh = q_r.shape[1:]
    nkeys = cache_k2.shape[1]
    rows = r_local.shape[1]
    assert dh == LANES and nq <= LANES and nq % 8 == 0
    scale = dh ** -0.5
    pps = DECODE_PAGES_PER_STEP if n_pages % DECODE_PAGES_PER_STEP == 0 else 1
    n_steps = n_pages // pps
    body = functools.partial(_fox_decode_body, heads=heads, n_new=n_new, n_steps=n_steps, pages_per_step=pps,
                             scale=scale)
    new_spec = pl.BlockSpec((None, nq, dh), lambda b, p, pt: (b, 0, 0))
    page_specs, page_args = [], []
    for i in range(pps):
        page = lambda b, p, pt, i=i: (pt[b, n_pages - 1 - (p * pps + i)], 0, 0)
        page_specs += [pl.BlockSpec((None, nkeys, dh), page), pl.BlockSpec((None, nkeys, dh), page),
                       pl.BlockSpec((None, rows, LANES), page), pl.BlockSpec((None, 1, LANES), page)]
        page_args += [cache_k2, cache_v2, r_local, tot]
    grid_spec = pltpu.PrefetchScalarGridSpec(
        num_scalar_prefetch=1,
        grid=(nb, n_steps),
        in_specs=[new_spec, new_spec, new_spec, new_spec, new_spec] + page_specs,
        out_specs=pl.BlockSpec((None, nq, dh), lambda b, p, pt: (b, 0, 0)),
        scratch_shapes=[pltpu.VMEM((nq, LANES), F32), pltpu.VMEM((nq, LANES), F32), pltpu.VMEM((nq, dh), F32),
                        pltpu.VMEM((8, LANES), F32), pltpu.VMEM((nq, LANES), F32), pltpu.VMEM((nq, nkeys), F32),
                        pltpu.VMEM((LANES, dh), F32), pltpu.VMEM((LANES, dh), F32)],
    )
    return pl.pallas_call(
        body,
        grid_spec=grid_spec,
        out_shape=jax.ShapeDtypeStruct((nb, nq, dh), BF16),
        compiler_params=_cparams("parallel", "arbitrary"),
        name="fox_decode",
    )(page_table, q_r, kn_r, vn_r, g_r, lfn_r, *page_args)


def _ffn_weights(w_in, w_out):
    d, f2 = w_in.shape
    f = f2 // 2
    fp = -(-f // FF_ALIGN) * FF_ALIGN
    wg = jnp.pad(w_in[:, :f], ((0, 0), (0, fp - f))).astype(BF16)
    wu = jnp.pad(w_in[:, f:], ((0, 0), (0, fp - f))).astype(BF16)
    wo = jnp.pad(w_out, ((0, fp - f), (0, 0))).astype(BF16)
    return wg, wu, wo


def kernel(x_prompt, x_sample, state_C, state_n, state_m, cache_k, cache_v, cache_logf, page_table,
           ln_g, ln_b, ffn_w_in, ffn_w_out, m_w_in, m_b_i, m_b_f, m_norm, m_w_out,
           f_w_in, f_b_f, f_q_norm, f_k_norm, f_w_out):
    nb_p, t, d = x_prompt.shape
    nb_s, t_s, _ = x_sample.shape
    depth = ln_g.shape[0]
    alpha = (2 * depth) ** 0.25
    n_p = nb_p * t
    n_s = nb_s * t_s
    assert n_s == CHUNK and t % CHUNK == 0 and (t_s & (t_s - 1)) == 0
    m_heads = m_b_i.shape[0]
    m_dqk = d // (2 * m_heads)
    m_dv = d // m_heads
    f_heads = f_b_f.shape[0]
    f_dh = d // f_heads
    nqk = m_heads * m_dqk
    nv = m_heads * m_dv

    x32 = jnp.concatenate([x_prompt.reshape(n_p, d), x_sample.reshape(n_s, d)], axis=0)
    x16 = x32.astype(BF16)

    def ffn(x32, x16, layer, which, g, b):
        wg, wu, wo = _ffn_weights(ffn_w_in[layer, which], ffn_w_out[layer, which])
        hid = _ffn_in(x16, wg, wu)
        return _mm_res_ln(hid, wo, x32, g, b, alpha=alpha, scale=0.5)

    outs = {}
    for layer in range(depth):
        g, b = ln_g[layer], ln_b[layer]
        x32, x16 = ffn(x32, x16, layer, 0, g[0], b[0])

        if layer % 2 == 0:
            n_main = 2 * nqk + 2 * nv
            colscale = jnp.concatenate([jnp.ones((nqk,), F32), jnp.full((nqk,), m_dqk ** -0.5, F32),
                                        jnp.ones((2 * nv,), F32)])
            z = _proj(x16, m_w_in[:, :n_main].astype(BF16), colscale)
            _, grow = _gates(x16, m_w_in[:, n_main:], jnp.concatenate([m_b_i, m_b_f]), ls_from=m_heads)
            common = dict(heads=m_heads, dqk=m_dqk, dv=m_dv)
            zero_state = (jnp.zeros((nb_p, m_heads, m_dv, m_dqk), F32), jnp.zeros((nb_p, m_heads, m_dqk), F32),
                          jnp.zeros((nb_p, m_heads, LANES), F32))
            hp, c_p, n_pr, m_p = _mlstm(z, grow, *zero_state, m_norm, row_block0=0, n_chunks=t // CHUNK,
                                        seg_shift=int(math.log2(CHUNK)), per_batch_seg=False, **common)
            m0_s = jnp.broadcast_to(state_m.astype(F32)[:, :, None], (nb_s, m_heads, LANES))
            hs, c_s, n_sm, m_s = _mlstm(z, grow, state_C.astype(F32), state_n.astype(F32), m0_s, m_norm,
                                        row_block0=n_p // CHUNK, n_chunks=1, seg_shift=int(math.log2(t_s)),
                                        per_batch_seg=True, **common)
            y16 = jnp.concatenate([hp, hs], axis=0)
            w_mix = m_w_out
            outs.update(c_p=c_p, n_p=n_pr, m_p=m_p[:, :, 0], c_s=c_s, n_s=n_sm, m_s=m_s[:, :, 0])
        else:
            n_main = 4 * d
            colscale = jnp.concatenate([jnp.tile(f_q_norm.astype(F32), f_heads), jnp.tile(f_k_norm.astype(F32), f_heads),
                                        jnp.ones((2 * d,), F32)])
            z = _proj(x16, f_w_in[:, :n_main].astype(BF16), colscale, norm_cols=2 * d, group=f_dh)
            lf_cols, lf_rows = _gates(x16, f_w_in[:, n_main:], f_b_f, ls_from=0)
            c3 = _cumsum_rows(lf_rows, nb_p, t).reshape(nb_p * f_heads, 1, t)
            yp = _fox_attn(z, c3, nb=nb_p, t=t, heads=f_heads, dh=f_dh)
            n_phys, page_len = cache_k.shape[:2]
            rows = page_len * f_heads // LANES
            r_local, tot = _logf_suffix(cache_logf.astype(F32).reshape(n_phys * rows, LANES), heads=f_heads, rows=rows)
            zs = z[n_p:]
            per_row = lambda a: a.reshape(nb_s, t_s * f_heads, f_dh)
            lfn_r = jnp.broadcast_to(lf_cols[n_p:, :f_heads].reshape(nb_s, t_s * f_heads, 1),
                                     (nb_s, t_s * f_heads, LANES))
            ys = _fox_decode(per_row(zs[:, :d]), per_row(zs[:, d:2 * d]), per_row(zs[:, 2 * d:3 * d]),
                             per_row(zs[:, 3 * d:]), lfn_r,
                             cache_k.reshape(n_phys, page_len * f_heads, f_dh),
                             cache_v.reshape(n_phys, page_len * f_heads, f_dh),
                             r_local.reshape(n_phys, rows, LANES), tot.reshape(n_phys, 1, LANES), page_table,
                             heads=f_heads, n_new=t_s)
            y16 = jnp.concatenate([yp, ys.reshape(n_s, d)], axis=0)
            w_mix = f_w_out
            kk = z[:, d:2 * d]
            vv = z[:, 2 * d:3 * d]
            lf = lf_cols[:, :f_heads]
            outs.update(k_p=kk[:n_p].reshape(nb_p, t, f_heads, f_dh), v_p=vv[:n_p].reshape(nb_p, t, f_heads, f_dh),
                        lf_p=lf[:n_p].reshape(nb_p, t, f_heads),
                        k_s=kk[n_p:].reshape(nb_s, t_s, f_heads, f_dh), v_s=vv[n_p:].reshape(nb_s, t_s, f_heads, f_dh),
                        lf_s=lf[n_p:].reshape(nb_s, t_s, f_heads))

        x32, x16 = _mm_res_ln(y16, w_mix.astype(BF16), x32, g[1], b[1], alpha=alpha, scale=1.0)
        x32, x16 = ffn(x32, x16, layer, 1, g[2], b[2])

    xp = x32[:n_p].reshape(nb_p, t, d)
    xs = x32[n_p:].reshape(nb_s, t_s, d)
    return (xp, xs, outs["c_p"], outs["n_p"], outs["m_p"], outs["c_s"], outs["n_s"], outs["m_s"],
            outs["k_p"], outs["v_p"], outs["lf_p"], outs["k_s"], outs["v_s"], outs["lf_s"])
```

```python
import functools
import math

import jax
import jax.numpy as jnp
from jax import lax
from jax.experimental import pallas as pl
from jax.experimental.pallas import tpu as pltpu

F32 = jnp.float32
BF16 = jnp.bfloat16
LN_EPS = 1e-5
NORM_EPS = 1e-6
LANES = 128
CHUNK = 128
PAGE = 128
FF_ALIGN = 512
DECODE_PAGES_PER_STEP = 8
VMEM_LIMIT = 56 * 1024 * 1024
NT_DIMS = (((1,), (1,)), ((), ()))


def _cparams(*sem):
    return pltpu.CompilerParams(dimension_semantics=sem, vmem_limit_bytes=VMEM_LIMIT)


def _pick(n, cands):
    for c in cands:
        if c <= n and n % c == 0:
            return c
    return n


def _log_sigmoid(x):
    return jnp.minimum(x, 0.0) - jnp.log1p(jnp.exp(-jnp.abs(x)))


def _dot01(x, m01):
    hi = x.astype(BF16)
    r1 = x - hi.astype(F32)
    mid = r1.astype(BF16)
    lo = (r1 - mid.astype(F32)).astype(BF16)
    d = lambda a: jnp.dot(a, m01, preferred_element_type=F32)
    return d(hi) + d(mid) + d(lo)


def _rows_to_cols(rows):
    r = rows.shape[0]
    if r < LANES:
        rows = jnp.concatenate([rows, jnp.zeros((LANES - r, LANES), F32)], axis=0)
    return rows.T


def _ffn_in_body(x_ref, wg_ref, wu_ref, h_ref, *, nf, shift):
    x = x_ref[...]
    g = jnp.dot(x, wg_ref[...].astype(BF16), preferred_element_type=F32)

    def finish(wu):
        u = jnp.dot(x, wu.astype(BF16), preferred_element_type=F32)
        h_ref[...] = (g * jax.nn.sigmoid(g) * u).astype(h_ref.dtype)

    lead0 = (0,) * (len(wu_ref.shape) - 2)
    if shift == 0:
        finish(wu_ref[lead0])
        return
    j = pl.program_id(1)

    @pl.when(j < nf - 1)
    def _():
        finish(wu_ref[lead0])

    @pl.when(j == nf - 1)
    def _():
        wu = wu_ref[lead0]
        finish(jnp.concatenate([wu[:, shift:], wu[:, :shift]], axis=1))


def _ffn_in(x16, w_in_all, lead):
    n, d = x16.shape
    f = w_in_all.shape[-1] // 2
    squeezed = (None,) * len(lead)
    tf = FF_ALIGN
    nf = -(-f // tf)
    shift = nf * tf - f
    assert shift % LANES == 0 and nf * tf <= 2 * f
    tm = _pick(n, (1664, 832, 640, 528, 512, 256, 128))
    body = functools.partial(_ffn_in_body, nf=nf, shift=shift)
    return pl.pallas_call(
        body,
        grid=(n // tm, nf),
        in_specs=[pl.BlockSpec((tm, d), lambda i, j: (i, 0)),
                  pl.BlockSpec(squeezed + (d, tf), lambda i, j: lead + (0, j)),
                  pl.BlockSpec(tuple(pl.Element(1) for _ in lead) + (pl.Element(d), pl.Element(tf)),
                               lambda i, j: lead + (0, LANES * jnp.minimum(f // LANES + j * (tf // LANES),
                                                                           (2 * f - tf) // LANES)))],
        out_specs=pl.BlockSpec((tm, tf), lambda i, j: (i, j)),
        out_shape=jax.ShapeDtypeStruct((n, nf * tf), BF16),
        compiler_params=_cparams("parallel", "arbitrary"),
        name="ffn_in",
    )(x16, w_in_all, w_in_all)


def _mm_res_ln_body(a_ref, w_ref, res_ref, g_ref, b_ref, o32_ref, o16_ref, *, alpha, scale):
    acc = jnp.dot(a_ref[...], w_ref[...], preferred_element_type=F32)
    z = alpha * res_ref[...] + scale * acc
    mu = jnp.mean(z, axis=-1, keepdims=True)
    zc = z - mu
    var = jnp.mean(zc * zc, axis=-1, keepdims=True)
    y = zc * lax.rsqrt(var + LN_EPS) * g_ref[...] + b_ref[...]
    o32_ref[...] = y
    o16_ref[...] = y.astype(BF16)


def _mm_res_ln(a16, w16, res32, g, b, *, alpha, scale):
    n = a16.shape[0]
    kdim, d = w16.shape
    assert kdim % LANES == 0 and kdim <= a16.shape[1]
    tm = _pick(n, (320, 256, 128))
    body = functools.partial(_mm_res_ln_body, alpha=alpha, scale=scale)
    return pl.pallas_call(
        body,
        grid=(n // tm,),
        in_specs=[pl.BlockSpec((tm, kdim), lambda i: (i, 0)),
                  pl.BlockSpec((kdim, d), lambda i: (0, 0), pipeline_mode=pl.Buffered(1)),
                  pl.BlockSpec((tm, d), lambda i: (i, 0)),
                  pl.BlockSpec((1, d), lambda i: (0, 0)),
                  pl.BlockSpec((1, d), lambda i: (0, 0))],
        out_specs=[pl.BlockSpec((tm, d), lambda i: (i, 0)),
                   pl.BlockSpec((tm, d), lambda i: (i, 0))],
        out_shape=[jax.ShapeDtypeStruct((n, d), F32), jax.ShapeDtypeStruct((n, d), BF16)],
        compiler_params=_cparams("parallel"),
        name="mm_res_ln",
    )(a16, w16, res32, g.reshape(1, d), b.reshape(1, d))


def _proj_body(x_ref, w_ref, cs_ref, o_ref, *, n_norm_tiles, group):
    z = jnp.dot(x_ref[...], w_ref[...].astype(BF16), preferred_element_type=F32)
    cs = cs_ref[...]
    if n_norm_tiles == 0:
        o_ref[...] = z * cs
        return
    j = pl.program_id(1)

    @pl.when(j >= n_norm_tiles)
    def _():
        o_ref[...] = z * cs

    @pl.when(j < n_norm_tiles)
    def _():
        for s in range(z.shape[1] // group):
            sl = slice(s * group, (s + 1) * group)
            zh = z[:, sl]
            ms = jnp.mean(zh * zh, axis=-1, keepdims=True)
            o_ref[:, sl] = zh * lax.rsqrt(ms + NORM_EPS) * cs[:, sl]


def _proj(x16, w, colscale, *, norm_cols=0, group=LANES):
    n, d = x16.shape
    e = colscale.shape[0]
    tm = _pick(n, (1664, 832, 640, 528, 512, 256, 128))
    tn = _pick(e, (512, 256, 128))
    body = functools.partial(_proj_body, n_norm_tiles=norm_cols // tn, group=group)
    return pl.pallas_call(
        body,
        grid=(n // tm, e // tn),
        in_specs=[pl.BlockSpec((tm, d), lambda i, j: (i, 0)),
                  pl.BlockSpec((d, tn), lambda i, j: (0, j)),
                  pl.BlockSpec((1, tn), lambda i, j: (0, j))],
        out_specs=pl.BlockSpec((tm, tn), lambda i, j: (i, j)),
        out_shape=jax.ShapeDtypeStruct((n, e), F32),
        compiler_params=_cparams("parallel", "arbitrary"),
        name="proj",
    )(x16, w, colscale.reshape(1, e))


def _gate_body(x_ref, wc_ref, bc_ref, oc_ref, or_ref, *, ls_from, n_gates):
    zc = jnp.dot(x_ref[...], wc_ref[...], preferred_element_type=F32) + bc_ref[...]
    lane = lax.broadcasted_iota(jnp.int32, zc.shape, 1)
    act = jnp.where((lane >= ls_from) & (lane < n_gates), _log_sigmoid(zc), zc)
    oc_ref[...] = act
    for r in range(zc.shape[0] // LANES):
        or_ref[:, r * LANES:(r + 1) * LANES] = act[r * LANES:(r + 1) * LANES, :].T[:n_gates]


def _gates(x16, w_gate, bias, *, ls_from):
    n, d = x16.shape
    ng = w_gate.shape[1]
    assert ng % 8 == 0 and ng <= LANES
    tm = _pick(n, (1664, 1280, 1024, 640, 512, 256, 128))
    assert tm % LANES == 0
    wc = jnp.pad(w_gate, ((0, 0), (0, LANES - ng))).astype(BF16)
    bc = jnp.pad(bias.astype(F32), (0, LANES - ng)).reshape(1, LANES)
    body = functools.partial(_gate_body, ls_from=ls_from, n_gates=ng)
    return pl.pallas_call(
        body,
        grid=(n // tm,),
        in_specs=[pl.BlockSpec((tm, d), lambda i: (i, 0)),
                  pl.BlockSpec((d, LANES), lambda i: (0, 0)),
                  pl.BlockSpec((1, LANES), lambda i: (0, 0))],
        out_specs=[pl.BlockSpec((tm, LANES), lambda i: (i, 0)),
                   pl.BlockSpec((ng, tm), lambda i: (0, i))],
        out_shape=[jax.ShapeDtypeStruct((n, LANES), F32), jax.ShapeDtypeStruct((ng, n), F32)],
        compiler_params=_cparams("parallel"),
        name="gates",
    )(x16, wc, bc)


def _mlstm_body(q_ref, k_ref, v_ref, o_ref, g_ref, c0_ref, n0_ref, m0_ref, nw_ref,
                h_ref, c_ref, n_ref, m_ref, *, heads, dqk, dv, seg_shift, per_batch_seg):
    b = pl.program_id(0)
    c = pl.program_id(1)
    L = CHUNK
    seg_len = 1 << seg_shift
    seg = b if per_batch_seg else 0

    @pl.when(c == 0)
    def _():
        c_ref[...] = c0_ref[...]
        n_ref[...] = n0_ref[...]
        m_ref[...] = m0_ref[...]

    if per_batch_seg:
        @pl.when(b == 0)
        def _():
            h_ref[...] = jnp.zeros(h_ref.shape, h_ref.dtype)

    li = lax.broadcasted_iota(jnp.int32, (1, L), 1)
    si = lax.broadcasted_iota(jnp.int32, (L, 1), 0)
    lseg = lax.shift_right_logical(li, seg_shift)
    sseg = lax.shift_right_logical(si, seg_shift)
    pair_ok = (lseg == sseg) & (li <= si)
    row_live = sseg == seg
    lane_live = lseg == seg
    last_lane = seg * seg_len + (seg_len - 1)
    cum01 = jnp.where((lax.shift_right_logical(si, seg_shift) == lseg) & (si <= li), 1.0, 0.0).astype(BF16)

    gates = g_ref[...]
    ig = gates[:heads]
    lf = gates[heads:]
    bcum = _dot01(lf, cum01)
    src = ig - bcum
    bcols = _rows_to_cols(bcum)

    for h in range(heads):
        b_col = bcols[:, h:h + 1]
        b_row = bcum[h:h + 1, :]
        m_prev = m_ref[h:h + 1, 0:1]
        dmat = jnp.where(pair_ok, b_col + src[h:h + 1, :], -jnp.inf)
        inter = b_col + m_prev
        m_t = jnp.maximum(jnp.max(dmat, axis=-1, keepdims=True), inter)
        w_intra = jnp.exp(dmat - m_t)
        w_inter = jnp.exp(inter - m_t)

        qh = q_ref[:, h * dqk:(h + 1) * dqk]
        q16 = qh.astype(BF16)
        k16 = k_ref[:, h * dqk:(h + 1) * dqk].astype(BF16)
        vh = v_ref[:, h * dv:(h + 1) * dv]
        v16 = vh.astype(BF16)
        c_st = c_ref[h]
        n_st = n_ref[h:h + 1, :]

        s = lax.dot_general(q16, k16, NT_DIMS, preferred_element_type=F32) * w_intra
        num = (w_inter * lax.dot_general(q16, c_st.astype(BF16), NT_DIMS, preferred_element_type=F32)
               + jnp.dot(s.astype(BF16), v16, preferred_element_type=F32))
        den = (w_inter * jnp.sum(qh * n_st, axis=-1, keepdims=True)
               + jnp.sum(s, axis=-1, keepdims=True))
        hv = num / jnp.maximum(jnp.abs(den), jnp.exp(-m_t))
        ms = jnp.mean(hv * hv, axis=-1, keepdims=True)
        hn = hv * lax.rsqrt(ms + NORM_EPS) * nw_ref[:, h * dv:(h + 1) * dv]
        out = (hn * jax.nn.sigmoid(o_ref[:, h * dv:(h + 1) * dv])).astype(h_ref.dtype)
        if per_batch_seg:
            out = jnp.where(row_live, out, h_ref[:, h * dv:(h + 1) * dv])
        h_ref[:, h * dv:(h + 1) * dv] = out

        b_last = jnp.sum(jnp.where(li == last_lane, b_row, 0.0), axis=-1, keepdims=True)
        dec = jnp.where(lane_live, b_last - b_row + ig[h:h + 1, :], -jnp.inf)
        m_new = jnp.maximum(b_last + m_prev, jnp.max(dec, axis=-1, keepdims=True))
        w_s = jnp.exp(dec - m_new)
        w_c = jnp.exp(b_last + m_prev - m_new)
        vw = (vh.T * w_s).astype(BF16)
        c_ref[h] = w_c * c_st + jnp.dot(vw, k16, preferred_element_type=F32)
        w_s8 = jnp.broadcast_to(w_s, (8, L)).astype(BF16)
        n_ref[h:h + 1, :] = w_c * n_st + jnp.dot(w_s8, k16, preferred_element_type=F32)[0:1]
        m_ref[h:h + 1, :] = jnp.broadcast_to(m_new, (1, LANES))


def _mlstm(z, grow, c0, n0, m0, norm_w, *, row_block0, n_chunks, seg_shift, per_batch_seg, heads, dqk, dv):
    nb = c0.shape[0]
    nqk = heads * dqk
    nv = heads * dv
    qk_blocks_before_v = (2 * nqk) // nv
    if per_batch_seg:
        rb = lambda b, c: row_block0
        n_rows = CHUNK
        out_rb = lambda b, c: 0
    else:
        rb = lambda b, c: row_block0 + b * n_chunks + c
        n_rows = nb * n_chunks * CHUNK
        out_rb = lambda b, c: b * n_chunks + c
    body = functools.partial(_mlstm_body, heads=heads, dqk=dqk, dv=dv, seg_shift=seg_shift,
                             per_batch_seg=per_batch_seg)
    return pl.pallas_call(
        body,
        grid=(nb, n_chunks),
        in_specs=[pl.BlockSpec((CHUNK, nqk), lambda b, c: (rb(b, c), 0)),
                  pl.BlockSpec((CHUNK, nqk), lambda b, c: (rb(b, c), 1)),
                  pl.BlockSpec((CHUNK, nv), lambda b, c: (rb(b, c), qk_blocks_before_v)),
                  pl.BlockSpec((CHUNK, nv), lambda b, c: (rb(b, c), qk_blocks_before_v + 1)),
                  pl.BlockSpec((2 * heads, CHUNK), lambda b, c: (0, rb(b, c))),
                  pl.BlockSpec((None, heads, dv, dqk), lambda b, c: (b, 0, 0, 0)),
                  pl.BlockSpec((None, heads, dqk), lambda b, c: (b, 0, 0)),
                  pl.BlockSpec((None, heads, LANES), lambda b, c: (b, 0, 0)),
                  pl.BlockSpec((1, nv), lambda b, c: (0, 0))],
        out_specs=[pl.BlockSpec((CHUNK, nv), lambda b, c: (out_rb(b, c), 0)),
                   pl.BlockSpec((None, heads, dv, dqk), lambda b, c: (b, 0, 0, 0)),
                   pl.BlockSpec((None, heads, dqk), lambda b, c: (b, 0, 0)),
                   pl.BlockSpec((None, heads, LANES), lambda b, c: (b, 0, 0))],
        out_shape=[jax.ShapeDtypeStruct((n_rows, nv), BF16),
                   jax.ShapeDtypeStruct((nb, heads, dv, dqk), F32),
                   jax.ShapeDtypeStruct((nb, heads, dqk), F32),
                   jax.ShapeDtypeStruct((nb, heads, LANES), F32)],
        compiler_params=_cparams("arbitrary", "arbitrary"),
        name="mlstm_sample" if per_batch_seg else "mlstm_prompt",
    )(z, z, z, z, grow, c0, n0, m0, norm_w.reshape(1, nv))


def _cumsum_body(lf_ref, o_ref):
    t = lf_ref.shape[1]
    si = lax.broadcasted_iota(jnp.int32, (LANES, LANES), 0)
    li = lax.broadcasted_iota(jnp.int32, (LANES, LANES), 1)
    incl = jnp.where(si <= li, 1.0, 0.0).astype(BF16)
    carry = jnp.zeros((lf_ref.shape[0], 1), F32)
    for r in range(t // LANES):
        blk = _dot01(lf_ref[:, r * LANES:(r + 1) * LANES], incl) + carry
        o_ref[:, r * LANES:(r + 1) * LANES] = blk
        carry = blk[:, LANES - 1:LANES]


def _cumsum_rows(lf_rows, nb, t):
    hh = lf_rows.shape[0]
    return pl.pallas_call(
        _cumsum_body,
        grid=(nb,),
        in_specs=[pl.BlockSpec((hh, t), lambda b: (0, b))],
        out_specs=pl.BlockSpec((None, hh, t), lambda b: (b, 0, 0)),
        out_shape=jax.ShapeDtypeStruct((nb, hh, t), F32),
        compiler_params=_cparams("parallel"),
        name="fox_cumsum",
    )(lf_rows)


def _fox_attn_body(q_ref, k_ref, v_ref, g_ref, cq_ref, ck_ref, o_ref, cqb_s, *, tq, nq, scale):
    qi = pl.program_id(2)
    for r in range(tq // LANES):
        row = cq_ref[:, r * LANES:(r + 1) * LANES]
        cqb_s[r * LANES:(r + 1) * LANES, :] = jnp.broadcast_to(row, (LANES, LANES)).T
    q16 = q_ref[...].astype(BF16)

    for j in range(nq):
        @pl.when(qi == j)
        def _(j=j):
            n_keys = (j + 1) * tq
            k16 = k_ref[0:n_keys, :].astype(BF16)
            v16 = v_ref[0:n_keys, :].astype(BF16)
            s = lax.dot_general(q16, k16, NT_DIMS, preferred_element_type=F32) * scale
            s = s + jnp.concatenate([cqb_s[...]] * (n_keys // LANES), axis=1) - ck_ref[:, 0:n_keys]
            rr = lax.broadcasted_iota(jnp.int32, (tq, tq), 0)
            cc = lax.broadcasted_iota(jnp.int32, (tq, tq), 1)
            tail = jnp.where(cc <= rr, s[:, j * tq:], -jnp.inf)
            s = tail if j == 0 else jnp.concatenate([s[:, :j * tq], tail], axis=1)
            m = jnp.max(s, axis=-1, keepdims=True)
            p = jnp.exp(s - m)
            den = jnp.sum(p, axis=-1, keepdims=True)
            o = jnp.dot(p.astype(BF16), v16, preferred_element_type=F32) / den
            o_ref[...] = (o * jax.nn.sigmoid(g_ref[...])).astype(o_ref.dtype)


def _fox_attn(z, c3, *, nb, t, heads, dh):
    tq = _pick(t, (512, 256, 128))
    nq = t // tq
    scale = dh ** -0.5
    body = functools.partial(_fox_attn_body, tq=tq, nq=nq, scale=scale)
    return pl.pallas_call(
        body,
        grid=(nb, heads, nq),
        in_specs=[pl.BlockSpec((tq, dh), lambda b, h, qi: (b * nq + qi, h)),
                  pl.BlockSpec((t, dh), lambda b, h, qi: (b, heads + h)),
                  pl.BlockSpec((t, dh), lambda b, h, qi: (b, 2 * heads + h)),
                  pl.BlockSpec((tq, dh), lambda b, h, qi: (b * nq + qi, 3 * heads + h)),
                  pl.BlockSpec((None, 1, tq), lambda b, h, qi: (b * heads + h, 0, qi)),
                  pl.BlockSpec((None, 1, t), lambda b, h, qi: (b * heads + h, 0, 0))],
        out_specs=pl.BlockSpec((tq, dh), lambda b, h, qi: (b * nq + qi, h)),
        out_shape=jax.ShapeDtypeStruct((nb * t, heads * dh), BF16),
        scratch_shapes=[pltpu.VMEM((tq, LANES), F32)],
        compiler_params=_cparams("parallel", "parallel", "arbitrary"),
        name="fox_attn",
    )(z, z, z, z, c3, c3)


def _dot01_left(m01, x):
    hi = x.astype(BF16)
    r1 = x - hi.astype(F32)
    mid = r1.astype(BF16)
    lo = (r1 - mid.astype(F32)).astype(BF16)
    d = lambda a: jnp.dot(m01, a, preferred_element_type=F32)
    return d(hi) + d(mid) + d(lo)


def _logf_suffix_body(lf_ref, r_ref, tot_ref, *, heads, rows):
    n = lf_ref.shape[0]
    pb = n // rows
    x = lf_ref[...]
    la = lax.broadcasted_iota(jnp.int32, (LANES, LANES), 0)
    lb = lax.broadcasted_iota(jnp.int32, (LANES, LANES), 1)
    same_head = (la % heads) == (lb % heads)
    later01 = jnp.where(same_head & (la > lb), 1.0, 0.0).astype(BF16)
    head01 = jnp.where(same_head, 1.0, 0.0).astype(BF16)
    within = _dot01(x, later01)
    row_tot = _dot01(x, head01)
    ra = lax.broadcasted_iota(jnp.int32, (LANES, LANES), 0)
    rb = lax.broadcasted_iota(jnp.int32, (LANES, LANES), 1)
    below01 = jnp.where((ra // rows == rb // rows) & (rb > ra), 1.0, 0.0).astype(BF16)
    for g in range(n // LANES):
        sl = slice(g * LANES, (g + 1) * LANES)
        r_ref[sl, :] = within[sl] + _dot01_left(below01, row_tot[sl])
    pa = lax.broadcasted_iota(jnp.int32, (pb, n), 0)
    pr = lax.broadcasted_iota(jnp.int32, (pb, n), 1)
    page01 = jnp.where(pr // rows == pa, 1.0, 0.0).astype(BF16)
    tot_ref[...] = _dot01_left(page01, row_tot)


def _logf_suffix(lf2, *, heads, rows):
    n_phys = lf2.shape[0] // rows
    pb = _pick(n_phys, (64, 32, 16, 8))
    assert (pb * rows) % LANES == 0 and LANES % rows == 0
    body = functools.partial(_logf_suffix_body, heads=heads, rows=rows)
    return pl.pallas_call(
        body,
        grid=(n_phys // pb,),
        in_specs=[pl.BlockSpec((pb * rows, LANES), lambda i: (i, 0))],
        out_specs=[pl.BlockSpec((pb * rows, LANES), lambda i: (i, 0)),
                   pl.BlockSpec((pb, LANES), lambda i: (i, 0))],
        out_shape=[jax.ShapeDtypeStruct((n_phys * rows, LANES), F32), jax.ShapeDtypeStruct((n_phys, LANES), F32)],
        compiler_params=_cparams("parallel"),
        name="logf_suffix",
    )(lf2)


def _fox_decode_body(pt_ref, q_ref, kn_ref, vn_ref, g_ref, lfn_ref, *rest, heads, n_new, n_steps, pages_per_step, scale):
    page_refs = rest[:4 * pages_per_step]
    o_ref, m_s, l_s, acc_s, carry_s, cn_s, mask_s, kn_s, vn_s = rest[4 * pages_per_step:]
    p = pl.program_id(1)
    nq = n_new * heads
    nkeys = page_refs[0].shape[0]
    rows_per_page = page_refs[2].shape[0]
    q16 = q_ref[...].astype(BF16)

    def update(lg, v16s):
        rep = lg.shape[1] // LANES
        width = lg.shape[1] // len(v16s)
        m_prev = m_s[...]
        m_new = jnp.maximum(m_prev, jnp.max(lg, axis=-1, keepdims=True))
        pp = jnp.exp(lg - jnp.concatenate([m_new] * rep, axis=1))
        corr = jnp.exp(m_prev - m_new)
        l_s[...] = corr * l_s[...] + jnp.sum(pp, axis=-1, keepdims=True)
        pv = jnp.dot(pp[:, :width].astype(BF16), v16s[0], preferred_element_type=F32)
        for i in range(1, len(v16s)):
            pv = pv + jnp.dot(pp[:, i * width:(i + 1) * width].astype(BF16), v16s[i], preferred_element_type=F32)
        acc_s[...] = corr * acc_s[...] + pv
        m_s[...] = m_new

    @pl.when(p == 0)
    def _():
        m_s[...] = jnp.full(m_s.shape, -jnp.inf, F32)
        l_s[...] = jnp.zeros(l_s.shape, F32)
        acc_s[...] = jnp.zeros(acc_s.shape, F32)
        carry_s[...] = jnp.zeros(carry_s.shape, F32)
        r_i = lax.broadcasted_iota(jnp.int32, (nq, nkeys), 0)
        c_i = lax.broadcasted_iota(jnp.int32, (nq, nkeys), 1)
        mask_s[...] = jnp.where((r_i % heads) == (c_i % heads), 0.0, -jnp.inf)
        ra = lax.broadcasted_iota(jnp.int32, (nq, nq), 0)
        rb = lax.broadcasted_iota(jnp.int32, (nq, nq), 1)
        upto01 = jnp.where(((ra % heads) == (rb % heads)) & (rb <= ra), 1.0, 0.0).astype(BF16)
        cn = _dot01_left(upto01, lfn_ref[...])
        cn_s[...] = cn
        kn_s[...] = jnp.zeros(kn_s.shape, F32)
        vn_s[...] = jnp.zeros(vn_s.shape, F32)
        kn_s[0:nq, :] = kn_ref[...]
        vn_s[0:nq, :] = vn_ref[...]
        s = lax.dot_general(q16, kn_s[...].astype(BF16), NT_DIMS, preferred_element_type=F32)
        cn_row = _rows_to_cols(cn)[0:1, :]
        rn = lax.broadcasted_iota(jnp.int32, (nq, LANES), 0)
        cc = lax.broadcasted_iota(jnp.int32, (nq, LANES), 1)
        ok = ((rn % heads) == (cc % heads)) & (cc <= rn) & (cc < nq)
        lg = jnp.where(ok, s * scale + cn - cn_row, -jnp.inf)
        update(lg, [vn_s[...].astype(BF16)])

    cn_full = jnp.concatenate([cn_s[...]] * rows_per_page, axis=1)
    carry = carry_s[0:1, :]
    lgs, v16s = [], []
    for i in range(pages_per_step):
        kp_ref, vp_ref, rl_ref, tot_ref = page_refs[4 * i:4 * i + 4]
        s = lax.dot_general(q16, kp_ref[...].astype(BF16), NT_DIMS, preferred_element_type=F32)
        bias = rl_ref[...] + carry
        bias_full = jnp.concatenate([jnp.broadcast_to(bias[j:j + 1, :], (nq, LANES)) for j in range(rows_per_page)],
                                    axis=1)
        lgs.append(s * scale + cn_full + bias_full + mask_s[...])
        v16s.append(vp_ref[...].astype(BF16))
        carry = carry + tot_ref[...]
    update(lgs[0] if pages_per_step == 1 else jnp.concatenate(lgs, axis=1), v16s)
    carry_s[...] = jnp.broadcast_to(carry, carry_s.shape)

    @pl.when(p == n_steps - 1)
    def _():
        o = acc_s[...] / l_s[...]
        o_ref[...] = (o * jax.nn.sigmoid(g_ref[...])).astype(o_ref.dtype)


def _fox_decode(q_r, kn_r, vn_r, g_r, lfn_r, cache_k2, cache_v2, r_local, tot, page_table, *, heads, n_new):
    nb, n_pages = page_table.shape
    nq, dh = q_r.shape[1:]
    nkeys = cache_k2.shape[1]
    rows = r_local.shape[1]
    assert dh == LANES and nq <= LANES and nq % 8 == 0
    scale = dh ** -0.5
    pps = DECODE_PAGES_PER_STEP if n_pages % DECODE_PAGES_PER_STEP == 0 else 1
    n_steps = n_pages // pps
    body = functools.partial(_fox_decode_body, heads=heads, n_new=n_new, n_steps=n_steps, pages_per_step=pps,
                             scale=scale)
    new_spec = pl.BlockSpec((None, nq, dh), lambda b, p, pt: (b, 0, 0))
    page_specs, page_args = [], []
    for i in range(pps):
        page = lambda b, p, pt, i=i: (pt[b, n_pages - 1 - (p * pps + i)], 0, 0)
        page_specs += [pl.BlockSpec((None, nkeys, dh), page), pl.BlockSpec((None, nkeys, dh), page),
                       pl.BlockSpec((None, rows, LANES), page), pl.BlockSpec((None, 1, LANES), page)]
        page_args += [cache_k2, cache_v2, r_local, tot]
    grid_spec = pltpu.PrefetchScalarGridSpec(
        num_scalar_prefetch=1,
        grid=(nb, n_steps),
        in_specs=[new_spec, new_spec, new_spec, new_spec, new_spec] + page_specs,
        out_specs=pl.BlockSpec((None, nq, dh), lambda b, p, pt: (b, 0, 0)),
        scratch_shapes=[pltpu.VMEM((nq, LANES), F32), pltpu.VMEM((nq, LANES), F32), pltpu.VMEM((nq, dh), F32),
                        pltpu.VMEM((8, LANES), F32), pltpu.VMEM((nq, LANES), F32), pltpu.VMEM((nq, nkeys), F32),
                        pltpu.VMEM((LANES, dh), F32), pltpu.VMEM((LANES, dh), F32)],
    )
    return pl.pallas_call(
        body,
        grid_spec=grid_spec,
        out_shape=jax.ShapeDtypeStruct((nb, nq, dh), BF16),
        compiler_params=_cparams("parallel", "arbitrary"),
        name="fox_decode",
    )(page_table, q_r, kn_r, vn_r, g_r, lfn_r, *page_args)


def kernel(x_prompt, x_sample, state_C, state_n, state_m, cache_k, cache_v, cache_logf, page_table,
           ln_g, ln_b, ffn_w_in, ffn_w_out, m_w_in, m_b_i, m_b_f, m_norm, m_w_out,
           f_w_in, f_b_f, f_q_norm, f_k_norm, f_w_out):
    nb_p, t, d = x_prompt.shape
    nb_s, t_s, _ = x_sample.shape
    depth = ln_g.shape[0]
    alpha = (2 * depth) ** 0.25
    n_p = nb_p * t
    n_s = nb_s * t_s
    assert n_s == CHUNK and t % CHUNK == 0 and (t_s & (t_s - 1)) == 0
    m_heads = m_b_i.shape[0]
    m_dqk = d // (2 * m_heads)
    m_dv = d // m_heads
    f_heads = f_b_f.shape[0]
    f_dh = d // f_heads
    nqk = m_heads * m_dqk
    nv = m_heads * m_dv

    x32 = jnp.concatenate([x_prompt.reshape(n_p, d), x_sample.reshape(n_s, d)], axis=0)
    x16 = x32.astype(BF16)

    def ffn(x32, x16, layer, which, g, b):
        hid = _ffn_in(x16, ffn_w_in, (layer, which))
        return _mm_res_ln(hid, ffn_w_out[layer, which].astype(BF16), x32, g, b, alpha=alpha, scale=0.5)

    outs = {}
    for layer in range(depth):
        g, b = ln_g[layer], ln_b[layer]
        x32, x16 = ffn(x32, x16, layer, 0, g[0], b[0])

        if layer % 2 == 0:
            n_main = 2 * nqk + 2 * nv
            colscale = jnp.concatenate([jnp.ones((nqk,), F32), jnp.full((nqk,), m_dqk ** -0.5, F32),
                                        jnp.ones((2 * nv,), F32)])
            z = _proj(x16, m_w_in, colscale)
            _, grow = _gates(x16, m_w_in[:, n_main:], jnp.concatenate([m_b_i, m_b_f]), ls_from=m_heads)
            common = dict(heads=m_heads, dqk=m_dqk, dv=m_dv)
            zero_state = (jnp.zeros((nb_p, m_heads, m_dv, m_dqk), F32), jnp.zeros((nb_p, m_heads, m_dqk), F32),
                          jnp.zeros((nb_p, m_heads, LANES), F32))
            hp, c_p, n_pr, m_p = _mlstm(z, grow, *zero_state, m_norm, row_block0=0, n_chunks=t // CHUNK,
                                        seg_shift=int(math.log2(CHUNK)), per_batch_seg=False, **common)
            m0_s = jnp.broadcast_to(state_m.astype(F32)[:, :, None], (nb_s, m_heads, LANES))
            hs, c_s, n_sm, m_s = _mlstm(z, grow, state_C.astype(F32), state_n.astype(F32), m0_s, m_norm,
                                        row_block0=n_p // CHUNK, n_chunks=1, seg_shift=int(math.log2(t_s)),
                                        per_batch_seg=True, **common)
            y16 = jnp.concatenate([hp, hs], axis=0)
            w_mix = m_w_out
            outs.update(c_p=c_p, n_p=n_pr, m_p=m_p[:, :, 0], c_s=c_s, n_s=n_sm, m_s=m_s[:, :, 0])
        else:
            n_main = 4 * d
            colscale = jnp.concatenate([jnp.tile(f_q_norm.astype(F32), f_heads), jnp.tile(f_k_norm.astype(F32), f_heads),
                                        jnp.ones((2 * d,), F32)])
            z = _proj(x16, f_w_in, colscale, norm_cols=2 * d, group=f_dh)
            lf_cols, lf_rows = _gates(x16, f_w_in[:, n_main:], f_b_f, ls_from=0)
            c3 = _cumsum_rows(lf_rows, nb_p, t).reshape(nb_p * f_heads, 1, t)
            yp = _fox_attn(z, c3, nb=nb_p, t=t, heads=f_heads, dh=f_dh)
            n_phys, page_len = cache_k.shape[:2]
            rows = page_len * f_heads // LANES
            r_local, tot = _logf_suffix(cache_logf.astype(F32).reshape(n_phys * rows, LANES), heads=f_heads, rows=rows)
            zs = z[n_p:]
            per_row = lambda a: a.reshape(nb_s, t_s * f_heads, f_dh)
            lfn_r = jnp.broadcast_to(lf_cols[n_p:, :f_heads].reshape(nb_s, t_s * f_heads, 1),
                                     (nb_s, t_s * f_heads, LANES))
            ys = _fox_decode(per_row(zs[:, :d]), per_row(zs[:, d:2 * d]), per_row(zs[:, 2 * d:3 * d]),
                             per_row(zs[:, 3 * d:]), lfn_r,
                             cache_k.reshape(n_phys, page_len * f_heads, f_dh),
                             cache_v.reshape(n_phys, page_len * f_heads, f_dh),
                             r_local.reshape(n_phys, rows, LANES), tot.reshape(n_phys, 1, LANES), page_table,
                             heads=f_heads, n_new=t_s)
            y16 = jnp.concatenate([yp, ys.reshape(n_s, d)], axis=0)
            w_mix = f_w_out
            kk = z[:, d:2 * d]
            vv = z[:, 2 * d:3 * d]
            lf = lf_cols[:, :f_heads]
            outs.update(k_p=kk[:n_p].reshape(nb_p, t, f_heads, f_dh), v_p=vv[:n_p].reshape(nb_p, t, f_heads, f_dh),
                        lf_p=lf[:n_p].reshape(nb_p, t, f_heads),
                        k_s=kk[n_p:].reshape(nb_s, t_s, f_heads, f_dh), v_s=vv[n_p:].reshape(nb_s, t_s, f_heads, f_dh),
                        lf_s=lf[n_p:].reshape(nb_s, t_s, f_heads))

        x32, x16 = _mm_res_ln(y16, w_mix.astype(BF16), x32, g[1], b[1], alpha=alpha, scale=1.0)
        x32, x16 = ffn(x32, x16, layer, 1, g[2], b[2])

    xp = x32[:n_p].reshape(nb_p, t, d)
    xs = x32[n_p:].reshape(nb_s, t_s, d)
    return (xp, xs, outs["c_p"], outs["n_p"], outs["m_p"], outs["c_s"], outs["n_s"], outs["m_s"],
            outs["k_p"], outs["v_p"], outs["lf_p"], outs["k_s"], outs["v_s"], outs["lf_s"])
```

```python
import functools
import math

import jax
import jax.numpy as jnp
from jax import lax
from jax.experimental import pallas as pl
from jax.experimental.pallas import tpu as pltpu

F32 = jnp.float32
BF16 = jnp.bfloat16
LN_EPS = 1e-5
NORM_EPS = 1e-6
LANES = 128
CHUNK = 128
PAGE = 128
FF_ALIGN = 512
DECODE_PAGES_PER_STEP = 8
VMEM_LIMIT = 56 * 1024 * 1024
NT_DIMS = (((1,), (1,)), ((), ()))
LOG2E = math.log2(math.e)


def _cparams(*sem):
    return pltpu.CompilerParams(dimension_semantics=sem, vmem_limit_bytes=VMEM_LIMIT)


def _pick(n, cands):
    for c in cands:
        if c <= n and n % c == 0:
            return c
    return n


def _log_sigmoid(x):
    return jnp.minimum(x, 0.0) - jnp.log1p(jnp.exp(-jnp.abs(x)))


def _dot01(x, m01):
    hi = x.astype(BF16)
    r1 = x - hi.astype(F32)
    mid = r1.astype(BF16)
    lo = (r1 - mid.astype(F32)).astype(BF16)
    d = lambda a: jnp.dot(a, m01, preferred_element_type=F32)
    return d(hi) + d(mid) + d(lo)


def _rows_to_cols(rows):
    r = rows.shape[0]
    if r < LANES:
        rows = jnp.concatenate([rows, jnp.zeros((LANES - r, LANES), F32)], axis=0)
    return rows.T


def _ffn_in_body(x_ref, wg_ref, wu_ref, h_ref, *, nf, shift):
    x = x_ref[...]
    g = jnp.dot(x, wg_ref[...].astype(BF16), preferred_element_type=F32)

    def finish(wu):
        u = jnp.dot(x, wu.astype(BF16), preferred_element_type=F32)
        h_ref[...] = (g * jax.nn.sigmoid(g) * u).astype(h_ref.dtype)

    lead0 = (0,) * (len(wu_ref.shape) - 2)
    if shift == 0:
        finish(wu_ref[lead0])
        return
    j = pl.program_id(1)

    @pl.when(j < nf - 1)
    def _():
        finish(wu_ref[lead0])

    @pl.when(j == nf - 1)
    def _():
        wu = wu_ref[lead0]
        finish(jnp.concatenate([wu[:, shift:], wu[:, :shift]], axis=1))


def _ffn_in(x16, w_in_all, lead):
    n, d = x16.shape
    f = w_in_all.shape[-1] // 2
    squeezed = (None,) * len(lead)
    tf = FF_ALIGN
    nf = -(-f // tf)
    shift = nf * tf - f
    assert shift % LANES == 0 and nf * tf <= 2 * f
    tm = _pick(n, (1664, 832, 640, 528, 512, 256, 128))
    body = functools.partial(_ffn_in_body, nf=nf, shift=shift)
    return pl.pallas_call(
        body,
        grid=(n // tm, nf),
        in_specs=[pl.BlockSpec((tm, d), lambda i, j: (i, 0)),
                  pl.BlockSpec(squeezed + (d, tf), lambda i, j: lead + (0, j)),
                  pl.BlockSpec(tuple(pl.Element(1) for _ in lead) + (pl.Element(d), pl.Element(tf)),
                               lambda i, j: lead + (0, LANES * jnp.minimum(f // LANES + j * (tf // LANES),
                                                                           (2 * f - tf) // LANES)))],
        out_specs=pl.BlockSpec((tm, tf), lambda i, j: (i, j)),
        out_shape=jax.ShapeDtypeStruct((n, nf * tf), BF16),
        compiler_params=_cparams("parallel", "arbitrary"),
        name="ffn_in",
    )(x16, w_in_all, w_in_all)


def _mm_res_ln_body(a_ref, w_ref, res_ref, g_ref, b_ref, o32_ref, o16_ref, *, alpha, scale):
    acc = jnp.dot(a_ref[...], w_ref[...], preferred_element_type=F32)
    z = alpha * res_ref[...] + scale * acc
    mu = jnp.mean(z, axis=-1, keepdims=True)
    zc = z - mu
    var = jnp.mean(zc * zc, axis=-1, keepdims=True)
    y = zc * lax.rsqrt(var + LN_EPS) * g_ref[...] + b_ref[...]
    o32_ref[...] = y
    o16_ref[...] = y.astype(BF16)


def _mm_res_ln(a16, w16, res32, g, b, *, alpha, scale, lead=()):
    n = a16.shape[0]
    kdim, d = w16.shape[-2:]
    assert kdim % LANES == 0 and kdim <= a16.shape[1]
    tm = _pick(n, (320, 256, 128))
    body = functools.partial(_mm_res_ln_body, alpha=alpha, scale=scale)
    return pl.pallas_call(
        body,
        grid=(n // tm,),
        in_specs=[pl.BlockSpec((tm, kdim), lambda i: (i, 0)),
                  pl.BlockSpec((None,) * len(lead) + (kdim, d), lambda i: lead + (0, 0),
                               pipeline_mode=pl.Buffered(1)),
                  pl.BlockSpec((tm, d), lambda i: (i, 0)),
                  pl.BlockSpec((1, d), lambda i: (0, 0)),
                  pl.BlockSpec((1, d), lambda i: (0, 0))],
        out_specs=[pl.BlockSpec((tm, d), lambda i: (i, 0)),
                   pl.BlockSpec((tm, d), lambda i: (i, 0))],
        out_shape=[jax.ShapeDtypeStruct((n, d), F32), jax.ShapeDtypeStruct((n, d), BF16)],
        compiler_params=_cparams("parallel"),
        name="mm_res_ln",
    )(a16, w16, res32, g.reshape(1, d), b.reshape(1, d))


def _proj_body(x_ref, wt_ref, cs_ref, o_ref, *, n_norm_tiles, group):
    z = lax.dot_general(x_ref[...], wt_ref[...].astype(BF16), NT_DIMS, preferred_element_type=F32)
    cs = cs_ref[...]
    if n_norm_tiles == 0:
        o_ref[...] = z * cs
        return
    j = pl.program_id(1)

    @pl.when(j >= n_norm_tiles)
    def _():
        o_ref[...] = z * cs

    @pl.when(j < n_norm_tiles)
    def _():
        for s in range(z.shape[1] // group):
            sl = slice(s * group, (s + 1) * group)
            zh = z[:, sl]
            ms = jnp.mean(zh * zh, axis=-1, keepdims=True)
            o_ref[:, sl] = zh * lax.rsqrt(ms + NORM_EPS) * cs[:, sl]


def _proj(x16, wt, colscale, *, norm_cols=0, group=LANES):
    n, d = x16.shape
    e = colscale.shape[0]
    tm = _pick(n, (1664, 832, 640, 528, 512, 256, 128))
    tn = _pick(e, (512, 256, 128))
    body = functools.partial(_proj_body, n_norm_tiles=norm_cols // tn, group=group)
    return pl.pallas_call(
        body,
        grid=(n // tm, e // tn),
        in_specs=[pl.BlockSpec((tm, d), lambda i, j: (i, 0)),
                  pl.BlockSpec((tn, d), lambda i, j: (j, 0)),
                  pl.BlockSpec((1, tn), lambda i, j: (0, j))],
        out_specs=pl.BlockSpec((tm, tn), lambda i, j: (i, j)),
        out_shape=jax.ShapeDtypeStruct((n, e), F32),
        compiler_params=_cparams("parallel", "arbitrary"),
        name="proj",
    )(x16, wt, colscale.reshape(1, e))


def _gate_body(x_ref, wc_ref, bc_ref, oc_ref, or_ref, *, ls_from, n_gates):
    zc = jnp.dot(x_ref[...], wc_ref[...], preferred_element_type=F32) + bc_ref[...]
    lane = lax.broadcasted_iota(jnp.int32, zc.shape, 1)
    act = jnp.where((lane >= ls_from) & (lane < n_gates), _log_sigmoid(zc), zc)
    oc_ref[...] = act
    for r in range(zc.shape[0] // LANES):
        or_ref[:, r * LANES:(r + 1) * LANES] = act[r * LANES:(r + 1) * LANES, :].T[:n_gates]


def _gates(x16, w_gate, bias, *, ls_from):
    n, d = x16.shape
    ng = w_gate.shape[1]
    assert ng % 8 == 0 and ng <= LANES
    tm = _pick(n, (1664, 1280, 1024, 640, 512, 256, 128))
    assert tm % LANES == 0
    wc = jnp.pad(w_gate, ((0, 0), (0, LANES - ng))).astype(BF16)
    bc = jnp.pad(bias.astype(F32), (0, LANES - ng)).reshape(1, LANES)
    body = functools.partial(_gate_body, ls_from=ls_from, n_gates=ng)
    return pl.pallas_call(
        body,
        grid=(n // tm,),
        in_specs=[pl.BlockSpec((tm, d), lambda i: (i, 0)),
                  pl.BlockSpec((d, LANES), lambda i: (0, 0)),
                  pl.BlockSpec((1, LANES), lambda i: (0, 0))],
        out_specs=[pl.BlockSpec((tm, LANES), lambda i: (i, 0)),
                   pl.BlockSpec((ng, tm), lambda i: (0, i))],
        out_shape=[jax.ShapeDtypeStruct((n, LANES), F32), jax.ShapeDtypeStruct((ng, n), F32)],
        compiler_params=_cparams("parallel"),
        name="gates",
    )(x16, wc, bc)


def _mlstm_body(q_ref, k_ref, v_ref, o_ref, g_ref, c0_ref, n0_ref, m0_ref, nw_ref,
                h_ref, c_ref, n_ref, m_ref, *, heads, dqk, dv, seg_shift, per_batch_seg):
    b = pl.program_id(0)
    c = pl.program_id(1)
    L = CHUNK
    seg_len = 1 << seg_shift
    seg = b if per_batch_seg else 0

    @pl.when(c == 0)
    def _():
        c_ref[...] = c0_ref[...]
        n_ref[...] = n0_ref[...]
        m_ref[...] = m0_ref[...]

    if per_batch_seg:
        @pl.when(b == 0)
        def _():
            h_ref[...] = jnp.zeros(h_ref.shape, h_ref.dtype)

    li = lax.broadcasted_iota(jnp.int32, (1, L), 1)
    si = lax.broadcasted_iota(jnp.int32, (L, 1), 0)
    lseg = lax.shift_right_logical(li, seg_shift)
    sseg = lax.shift_right_logical(si, seg_shift)
    pair_ok = (lseg == sseg) & (li <= si)
    row_live = sseg == seg
    lane_live = lseg == seg
    last_lane = seg * seg_len + (seg_len - 1)
    cum01 = jnp.where((lax.shift_right_logical(si, seg_shift) == lseg) & (si <= li), 1.0, 0.0).astype(BF16)

    gates = g_ref[...]
    ig = gates[:heads]
    lf = gates[heads:]
    bcum = _dot01(lf, cum01)
    src = ig - bcum
    bcols = _rows_to_cols(bcum)

    for h in range(heads):
        b_col = bcols[:, h:h + 1]
        b_row = bcum[h:h + 1, :]
        m_prev = m_ref[h:h + 1, 0:1]
        dmat = jnp.where(pair_ok, b_col + src[h:h + 1, :], -jnp.inf)
        inter = b_col + m_prev
        m_t = jnp.maximum(jnp.max(dmat, axis=-1, keepdims=True), inter)
        w_intra = jnp.exp(dmat - m_t)
        w_inter = jnp.exp(inter - m_t)

        qh = q_ref[:, h * dqk:(h + 1) * dqk]
        q16 = qh.astype(BF16)
        k16 = k_ref[:, h * dqk:(h + 1) * dqk].astype(BF16)
        vh = v_ref[:, h * dv:(h + 1) * dv]
        v16 = vh.astype(BF16)
        c_st = c_ref[h]
        n_st = n_ref[h:h + 1, :]

        s = lax.dot_general(q16, k16, NT_DIMS, preferred_element_type=F32) * w_intra
        num = (w_inter * lax.dot_general(q16, c_st.astype(BF16), NT_DIMS, preferred_element_type=F32)
               + jnp.dot(s.astype(BF16), v16, preferred_element_type=F32))
        den = (w_inter * jnp.sum(qh * n_st, axis=-1, keepdims=True)
               + jnp.sum(s, axis=-1, keepdims=True))
        hv = num / jnp.maximum(jnp.abs(den), jnp.exp(-m_t))
        ms = jnp.mean(hv * hv, axis=-1, keepdims=True)
        hn = hv * lax.rsqrt(ms + NORM_EPS) * nw_ref[:, h * dv:(h + 1) * dv]
        out = (hn * jax.nn.sigmoid(o_ref[:, h * dv:(h + 1) * dv])).astype(h_ref.dtype)
        if per_batch_seg:
            out = jnp.where(row_live, out, h_ref[:, h * dv:(h + 1) * dv])
        h_ref[:, h * dv:(h + 1) * dv] = out

        b_last = jnp.sum(jnp.where(li == last_lane, b_row, 0.0), axis=-1, keepdims=True)
        dec = jnp.where(lane_live, b_last - b_row + ig[h:h + 1, :], -jnp.inf)
        m_new = jnp.maximum(b_last + m_prev, jnp.max(dec, axis=-1, keepdims=True))
        w_s = jnp.exp(dec - m_new)
        w_c = jnp.exp(b_last + m_prev - m_new)
        vw = (vh.T * w_s).astype(BF16)
        c_ref[h] = w_c * c_st + jnp.dot(vw, k16, preferred_element_type=F32)
        w_s8 = jnp.broadcast_to(w_s, (8, L)).astype(BF16)
        n_ref[h:h + 1, :] = w_c * n_st + jnp.dot(w_s8, k16, preferred_element_type=F32)[0:1]
        m_ref[h:h + 1, :] = jnp.broadcast_to(m_new, (1, LANES))


def _mlstm(z, grow, c0, n0, m0, norm_w, *, row_block0, n_chunks, seg_shift, per_batch_seg, heads, dqk, dv):
    nb = c0.shape[0]
    nqk = heads * dqk
    nv = heads * dv
    qk_blocks_before_v = (2 * nqk) // nv
    if per_batch_seg:
        rb = lambda b, c: row_block0
        n_rows = CHUNK
        out_rb = lambda b, c: 0
    else:
        rb = lambda b, c: row_block0 + b * n_chunks + c
        n_rows = nb * n_chunks * CHUNK
        out_rb = lambda b, c: b * n_chunks + c
    body = functools.partial(_mlstm_body, heads=heads, dqk=dqk, dv=dv, seg_shift=seg_shift,
                             per_batch_seg=per_batch_seg)
    return pl.pallas_call(
        body,
        grid=(nb, n_chunks),
        in_specs=[pl.BlockSpec((CHUNK, nqk), lambda b, c: (rb(b, c), 0)),
                  pl.BlockSpec((CHUNK, nqk), lambda b, c: (rb(b, c), 1)),
                  pl.BlockSpec((CHUNK, nv), lambda b, c: (rb(b, c), qk_blocks_before_v)),
                  pl.BlockSpec((CHUNK, nv), lambda b, c: (rb(b, c), qk_blocks_before_v + 1)),
                  pl.BlockSpec((2 * heads, CHUNK), lambda b, c: (0, rb(b, c))),
                  pl.BlockSpec((None, heads, dv, dqk), lambda b, c: (b, 0, 0, 0)),
                  pl.BlockSpec((None, heads, dqk), lambda b, c: (b, 0, 0)),
                  pl.BlockSpec((None, heads, LANES), lambda b, c: (b, 0, 0)),
                  pl.BlockSpec((1, nv), lambda b, c: (0, 0))],
        out_specs=[pl.BlockSpec((CHUNK, nv), lambda b, c: (out_rb(b, c), 0)),
                   pl.BlockSpec((None, heads, dv, dqk), lambda b, c: (b, 0, 0, 0)),
                   pl.BlockSpec((None, heads, dqk), lambda b, c: (b, 0, 0)),
                   pl.BlockSpec((None, heads, LANES), lambda b, c: (b, 0, 0))],
        out_shape=[jax.ShapeDtypeStruct((n_rows, nv), BF16),
                   jax.ShapeDtypeStruct((nb, heads, dv, dqk), F32),
                   jax.ShapeDtypeStruct((nb, heads, dqk), F32),
                   jax.ShapeDtypeStruct((nb, heads, LANES), F32)],
        compiler_params=_cparams("arbitrary", "arbitrary"),
        name="mlstm_sample" if per_batch_seg else "mlstm_prompt",
    )(z, z, z, z, grow, c0, n0, m0, norm_w.reshape(1, nv))


def _cumsum_body(lf_ref, o_ref):
    t = lf_ref.shape[1]
    si = lax.broadcasted_iota(jnp.int32, (LANES, LANES), 0)
    li = lax.broadcasted_iota(jnp.int32, (LANES, LANES), 1)
    incl = jnp.where(si <= li, 1.0, 0.0).astype(BF16)
    carry = jnp.zeros((lf_ref.shape[0], 1), F32)
    for r in range(t // LANES):
        blk = _dot01(lf_ref[:, r * LANES:(r + 1) * LANES], incl) + carry
        o_ref[:, r * LANES:(r + 1) * LANES] = blk
        carry = blk[:, LANES - 1:LANES]


def _cumsum_rows(lf_rows, nb, t):
    hh = lf_rows.shape[0]
    return pl.pallas_call(
        _cumsum_body,
        grid=(nb,),
        in_specs=[pl.BlockSpec((hh, t), lambda b: (0, b))],
        out_specs=pl.BlockSpec((None, hh, t), lambda b: (b, 0, 0)),
        out_shape=jax.ShapeDtypeStruct((nb, hh, t), F32),
        compiler_params=_cparams("parallel"),
        name="fox_cumsum",
    )(lf_rows)


def _fox_attn_body(q_ref, k_ref, v_ref, g_ref, cq_ref, ck_ref, o_ref, cqb_s, *, tq, nq, scale):
    qi = pl.program_id(2)
    for r in range(tq // LANES):
        row = cq_ref[:, r * LANES:(r + 1) * LANES]
        cqb_s[r * LANES:(r + 1) * LANES, :] = jnp.broadcast_to(row * LOG2E, (LANES, LANES)).T
    q16 = q_ref[...].astype(BF16)

    for j in range(nq):
        @pl.when(qi == j)
        def _(j=j):
            n_keys = (j + 1) * tq
            k16 = k_ref[0:n_keys, :].astype(BF16)
            v16 = v_ref[0:n_keys, :].astype(BF16)
            s = lax.dot_general(q16, k16, NT_DIMS, preferred_element_type=F32) * (scale * LOG2E)
            s = s + jnp.concatenate([cqb_s[...]] * (n_keys // LANES), axis=1) - ck_ref[:, 0:n_keys] * LOG2E
            rr = lax.broadcasted_iota(jnp.int32, (tq, tq), 0)
            cc = lax.broadcasted_iota(jnp.int32, (tq, tq), 1)
            tail = jnp.where(cc <= rr, s[:, j * tq:], -jnp.inf)
            s = tail if j == 0 else jnp.concatenate([s[:, :j * tq], tail], axis=1)
            m = jnp.max(s, axis=-1, keepdims=True)
            p = jnp.exp2(s - m)
            den = jnp.sum(p, axis=-1, keepdims=True)
            o = jnp.dot(p.astype(BF16), v16, preferred_element_type=F32) / den
            o_ref[...] = (o * jax.nn.sigmoid(g_ref[...])).astype(o_ref.dtype)


def _fox_attn(z, c3, *, nb, t, heads, dh):
    tq = _pick(t, (512, 256, 128))
    nq = t // tq
    scale = dh ** -0.5
    body = functools.partial(_fox_attn_body, tq=tq, nq=nq, scale=scale)
    return pl.pallas_call(
        body,
        grid=(nb, heads, nq),
        in_specs=[pl.BlockSpec((tq, dh), lambda b, h, qi: (b * nq + qi, h)),
                  pl.BlockSpec((t, dh), lambda b, h, qi: (b, heads + h)),
                  pl.BlockSpec((t, dh), lambda b, h, qi: (b, 2 * heads + h)),
                  pl.BlockSpec((tq, dh), lambda b, h, qi: (b * nq + qi, 3 * heads + h)),
                  pl.BlockSpec((None, 1, tq), lambda b, h, qi: (b * heads + h, 0, qi)),
                  pl.BlockSpec((None, 1, t), lambda b, h, qi: (b * heads + h, 0, 0))],
        out_specs=pl.BlockSpec((tq, dh), lambda b, h, qi: (b * nq + qi, h)),
        out_shape=jax.ShapeDtypeStruct((nb * t, heads * dh), BF16),
        scratch_shapes=[pltpu.VMEM((tq, LANES), F32)],
        compiler_params=_cparams("parallel", "parallel", "arbitrary"),
        name="fox_attn",
    )(z, z, z, z, c3, c3)


def _dot01_left(m01, x):
    hi = x.astype(BF16)
    r1 = x - hi.astype(F32)
    mid = r1.astype(BF16)
    lo = (r1 - mid.astype(F32)).astype(BF16)
    d = lambda a: jnp.dot(m01, a, preferred_element_type=F32)
    return d(hi) + d(mid) + d(lo)


def _logf_suffix_body(lf_ref, r_ref, tot_ref, *, heads, rows):
    n = lf_ref.shape[0]
    pb = n // rows
    x = lf_ref[...]
    la = lax.broadcasted_iota(jnp.int32, (LANES, LANES), 0)
    lb = lax.broadcasted_iota(jnp.int32, (LANES, LANES), 1)
    same_head = (la % heads) == (lb % heads)
    later01 = jnp.where(same_head & (la > lb), 1.0, 0.0).astype(BF16)
    head01 = jnp.where(same_head, 1.0, 0.0).astype(BF16)
    within = _dot01(x, later01)
    row_tot = _dot01(x, head01)
    ra = lax.broadcasted_iota(jnp.int32, (LANES, LANES), 0)
    rb = lax.broadcasted_iota(jnp.int32, (LANES, LANES), 1)
    below01 = jnp.where((ra // rows == rb // rows) & (rb > ra), 1.0, 0.0).astype(BF16)
    for g in range(n // LANES):
        sl = slice(g * LANES, (g + 1) * LANES)
        r_ref[sl, :] = within[sl] + _dot01_left(below01, row_tot[sl])
    pa = lax.broadcasted_iota(jnp.int32, (pb, n), 0)
    pr = lax.broadcasted_iota(jnp.int32, (pb, n), 1)
    page01 = jnp.where(pr // rows == pa, 1.0, 0.0).astype(BF16)
    tot_ref[...] = _dot01_left(page01, row_tot)


def _logf_suffix(lf2, *, heads, rows):
    n_phys = lf2.shape[0] // rows
    pb = _pick(n_phys, (64, 32, 16, 8))
    assert (pb * rows) % LANES == 0 and LANES % rows == 0
    body = functools.partial(_logf_suffix_body, heads=heads, rows=rows)
    return pl.pallas_call(
        body,
        grid=(n_phys // pb,),
        in_specs=[pl.BlockSpec((pb * rows, LANES), lambda i: (i, 0))],
        out_specs=[pl.BlockSpec((pb * rows, LANES), lambda i: (i, 0)),
                   pl.BlockSpec((pb, LANES), lambda i: (i, 0))],
        out_shape=[jax.ShapeDtypeStruct((n_phys * rows, LANES), F32), jax.ShapeDtypeStruct((n_phys, LANES), F32)],
        compiler_params=_cparams("parallel"),
        name="logf_suffix",
    )(lf2)


def _fox_decode_body(pt_ref, q_ref, kn_ref, vn_ref, g_ref, lfn_ref, *rest, heads, n_new, n_steps, pages_per_step, scale):
    page_refs = rest[:4 * pages_per_step]
    o_ref, m_s, l_s, acc_s, carry_s, cn_s, mask_s, kn_s, vn_s = rest[4 * pages_per_step:]
    p = pl.program_id(1)
    nq = n_new * heads
    nkeys = page_refs[0].shape[0]
    rows_per_page = page_refs[2].shape[0]
    q16 = q_ref[...].astype(BF16)

    def update(lg, v16s):
        rep = lg.shape[1] // LANES
        width = lg.shape[1] // len(v16s)
        m_prev = m_s[...]
        m_new = jnp.maximum(m_prev, jnp.max(lg, axis=-1, keepdims=True))
        pp = jnp.exp(lg - jnp.concatenate([m_new] * rep, axis=1))
        corr = jnp.exp(m_prev - m_new)
        l_s[...] = corr * l_s[...] + jnp.sum(pp, axis=-1, keepdims=True)
        pv = jnp.dot(pp[:, :width].astype(BF16), v16s[0], preferred_element_type=F32)
        for i in range(1, len(v16s)):
            pv = pv + jnp.dot(pp[:, i * width:(i + 1) * width].astype(BF16), v16s[i], preferred_element_type=F32)
        acc_s[...] = corr * acc_s[...] + pv
        m_s[...] = m_new

    @pl.when(p == 0)
    def _():
        m_s[...] = jnp.full(m_s.shape, -jnp.inf, F32)
        l_s[...] = jnp.zeros(l_s.shape, F32)
        acc_s[...] = jnp.zeros(acc_s.shape, F32)
        carry_s[...] = jnp.zeros(carry_s.shape, F32)
        r_i = lax.broadcasted_iota(jnp.int32, (nq, nkeys), 0)
        c_i = lax.broadcasted_iota(jnp.int32, (nq, nkeys), 1)
        mask_s[...] = jnp.where((r_i % heads) == (c_i % heads), 0.0, -jnp.inf)
        ra = lax.broadcasted_iota(jnp.int32, (nq, nq), 0)
        rb = lax.broadcasted_iota(jnp.int32, (nq, nq), 1)
        upto01 = jnp.where(((ra % heads) == (rb % heads)) & (rb <= ra), 1.0, 0.0).astype(BF16)
        cn = _dot01_left(upto01, lfn_ref[...])
        cn_s[...] = cn
        kn_s[...] = jnp.zeros(kn_s.shape, F32)
        vn_s[...] = jnp.zeros(vn_s.shape, F32)
        kn_s[0:nq, :] = kn_ref[...]
        vn_s[0:nq, :] = vn_ref[...]
        s = lax.dot_general(q16, kn_s[...].astype(BF16), NT_DIMS, preferred_element_type=F32)
        cn_row = _rows_to_cols(cn)[0:1, :]
        rn = lax.broadcasted_iota(jnp.int32, (nq, LANES), 0)
        cc = lax.broadcasted_iota(jnp.int32, (nq, LANES), 1)
        ok = ((rn % heads) == (cc % heads)) & (cc <= rn) & (cc < nq)
        lg = jnp.where(ok, s * scale + cn - cn_row, -jnp.inf)
        update(lg, [vn_s[...].astype(BF16)])

    cn_full = jnp.concatenate([cn_s[...]] * rows_per_page, axis=1)
    carry = carry_s[0:1, :]
    lgs, v16s = [], []
    for i in range(pages_per_step):
        kp_ref, vp_ref, rl_ref, tot_ref = page_refs[4 * i:4 * i + 4]
        s = lax.dot_general(q16, kp_ref[...].astype(BF16), NT_DIMS, preferred_element_type=F32)
        bias = rl_ref[...] + carry
        bias_full = jnp.concatenate([jnp.broadcast_to(bias[j:j + 1, :], (nq, LANES)) for j in range(rows_per_page)],
                                    axis=1)
        lgs.append(s * scale + cn_full + bias_full + mask_s[...])
        v16s.append(vp_ref[...].astype(BF16))
        carry = carry + tot_ref[...]
    update(lgs[0] if pages_per_step == 1 else jnp.concatenate(lgs, axis=1), v16s)
    carry_s[...] = jnp.broadcast_to(carry, carry_s.shape)

    @pl.when(p == n_steps - 1)
    def _():
        o = acc_s[...] / l_s[...]
        o_ref[...] = (o * jax.nn.sigmoid(g_ref[...])).astype(o_ref.dtype)


def _fox_decode(q_r, kn_r, vn_r, g_r, lfn_r, cache_k2, cache_v2, r_local, tot, page_table, *, heads, n_new):
    nb, n_pages = page_table.shape
    nq, dh = q_r.shape[1:]
    nkeys = cache_k2.shape[1]
    rows = r_local.shape[1]
    assert dh == LANES and nq <= LANES and nq % 8 == 0
    scale = dh ** -0.5
    pps = DECODE_PAGES_PER_STEP if n_pages % DECODE_PAGES_PER_STEP == 0 else 1
    n_steps = n_pages // pps
    body = functools.partial(_fox_decode_body, heads=heads, n_new=n_new, n_steps=n_steps, pages_per_step=pps,
                             scale=scale)
    new_spec = pl.BlockSpec((None, nq, dh), lambda b, p, pt: (b, 0, 0))
    page_specs, page_args = [], []
    for i in range(pps):
        page = lambda b, p, pt, i=i: (pt[b, n_pages - 1 - (p * pps + i)], 0, 0)
        page_specs += [pl.BlockSpec((None, nkeys, dh), page), pl.BlockSpec((None, nkeys, dh), page),
                       pl.BlockSpec((None, rows, LANES), page), pl.BlockSpec((None, 1, LANES), page)]
        page_args += [cache_k2, cache_v2, r_local, tot]
    grid_spec = pltpu.PrefetchScalarGridSpec(
        num_scalar_prefetch=1,
        grid=(nb, n_steps),
        in_specs=[new_spec, new_spec, new_spec, new_spec, new_spec] + page_specs,
        out_specs=pl.BlockSpec((None, nq, dh), lambda b, p, pt: (b, 0, 0)),
        scratch_shapes=[pltpu.VMEM((nq, LANES), F32), pltpu.VMEM((nq, LANES), F32), pltpu.VMEM((nq, dh), F32),
                        pltpu.VMEM((8, LANES), F32), pltpu.VMEM((nq, LANES), F32), pltpu.VMEM((nq, nkeys), F32),
                        pltpu.VMEM((LANES, dh), F32), pltpu.VMEM((LANES, dh), F32)],
    )
    return pl.pallas_call(
        body,
        grid_spec=grid_spec,
        out_shape=jax.ShapeDtypeStruct((nb, nq, dh), BF16),
        compiler_params=_cparams("parallel", "arbitrary"),
        name="fox_decode",
    )(page_table, q_r, kn_r, vn_r, g_r, lfn_r, *page_args)


def kernel(x_prompt, x_sample, state_C, state_n, state_m, cache_k, cache_v, cache_logf, page_table,
           ln_g, ln_b, ffn_w_in, ffn_w_out, m_w_in, m_b_i, m_b_f, m_norm, m_w_out,
           f_w_in, f_b_f, f_q_norm, f_k_norm, f_w_out):
    nb_p, t, d = x_prompt.shape
    nb_s, t_s, _ = x_sample.shape
    depth = ln_g.shape[0]
    alpha = (2 * depth) ** 0.25
    n_p = nb_p * t
    n_s = nb_s * t_s
    assert n_s == CHUNK and t % CHUNK == 0 and (t_s & (t_s - 1)) == 0
    m_heads = m_b_i.shape[0]
    m_dqk = d // (2 * m_heads)
    m_dv = d // m_heads
    f_heads = f_b_f.shape[0]
    f_dh = d // f_heads
    nqk = m_heads * m_dqk
    nv = m_heads * m_dv

    x32 = jnp.concatenate([x_prompt.reshape(n_p, d), x_sample.reshape(n_s, d)], axis=0)
    x16 = x32.astype(BF16)

    ffn_w_out16 = ffn_w_out.astype(BF16)

    def ffn(x32, x16, layer, which, g, b):
        hid = _ffn_in(x16, ffn_w_in, (layer, which))
        return _mm_res_ln(hid, ffn_w_out16, x32, g, b, alpha=alpha, scale=0.5, lead=(layer, which))

    outs = {}
    for layer in range(depth):
        g, b = ln_g[layer], ln_b[layer]
        x32, x16 = ffn(x32, x16, layer, 0, g[0], b[0])

        if layer % 2 == 0:
            n_main = 2 * nqk + 2 * nv
            colscale = jnp.concatenate([jnp.ones((nqk,), F32), jnp.full((nqk,), m_dqk ** -0.5, F32),
                                        jnp.ones((2 * nv,), F32)])
            z = _proj(x16, m_w_in.T, colscale)
            _, grow = _gates(x16, m_w_in[:, n_main:], jnp.concatenate([m_b_i, m_b_f]), ls_from=m_heads)
            common = dict(heads=m_heads, dqk=m_dqk, dv=m_dv)
            zero_state = (jnp.zeros((nb_p, m_heads, m_dv, m_dqk), F32), jnp.zeros((nb_p, m_heads, m_dqk), F32),
                          jnp.zeros((nb_p, m_heads, LANES), F32))
            hp, c_p, n_pr, m_p = _mlstm(z, grow, *zero_state, m_norm, row_block0=0, n_chunks=t // CHUNK,
                                        seg_shift=int(math.log2(CHUNK)), per_batch_seg=False, **common)
            m0_s = jnp.broadcast_to(state_m.astype(F32)[:, :, None], (nb_s, m_heads, LANES))
            hs, c_s, n_sm, m_s = _mlstm(z, grow, state_C.astype(F32), state_n.astype(F32), m0_s, m_norm,
                                        row_block0=n_p // CHUNK, n_chunks=1, seg_shift=int(math.log2(t_s)),
                                        per_batch_seg=True, **common)
            y16 = jnp.concatenate([hp, hs], axis=0)
            w_mix = m_w_out
            outs.update(c_p=c_p, n_p=n_pr, m_p=m_p[:, :, 0], c_s=c_s, n_s=n_sm, m_s=m_s[:, :, 0])
        else:
            n_main = 4 * d
            colscale = jnp.concatenate([jnp.tile(f_q_norm.astype(F32), f_heads), jnp.tile(f_k_norm.astype(F32), f_heads),
                                        jnp.ones((2 * d,), F32)])
            z = _proj(x16, f_w_in.T, colscale, norm_cols=2 * d, group=f_dh)
            lf_cols, lf_rows = _gates(x16, f_w_in[:, n_main:], f_b_f, ls_from=0)
            c3 = _cumsum_rows(lf_rows, nb_p, t).reshape(nb_p * f_heads, 1, t)
            yp = _fox_attn(z, c3, nb=nb_p, t=t, heads=f_heads, dh=f_dh)
            n_phys, page_len = cache_k.shape[:2]
            rows = page_len * f_heads // LANES
            r_local, tot = _logf_suffix(cache_logf.astype(F32).reshape(n_phys * rows, LANES), heads=f_heads, rows=rows)
            zs = z[n_p:]
            per_row = lambda a: a.reshape(nb_s, t_s * f_heads, f_dh)
            lfn_r = jnp.broadcast_to(lf_cols[n_p:, :f_heads].reshape(nb_s, t_s * f_heads, 1),
                                     (nb_s, t_s * f_heads, LANES))
            ys = _fox_decode(per_row(zs[:, :d]), per_row(zs[:, d:2 * d]), per_row(zs[:, 2 * d:3 * d]),
                             per_row(zs[:, 3 * d:]), lfn_r,
                             cache_k.reshape(n_phys, page_len * f_heads, f_dh),
                             cache_v.reshape(n_phys, page_len * f_heads, f_dh),
                             r_local.reshape(n_phys, rows, LANES), tot.reshape(n_phys, 1, LANES), page_table,
                             heads=f_heads, n_new=t_s)
            y16 = jnp.concatenate([yp, ys.reshape(n_s, d)], axis=0)
            w_mix = f_w_out
            kk = z[:, d:2 * d]
            vv = z[:, 2 * d:3 * d]
            lf = lf_cols[:, :f_heads]
            outs.update(k_p=kk[:n_p].reshape(nb_p, t, f_heads, f_dh), v_p=vv[:n_p].reshape(nb_p, t, f_heads, f_dh),
                        lf_p=lf[:n_p].reshape(nb_p, t, f_heads),
                        k_s=kk[n_p:].reshape(nb_s, t_s, f_heads, f_dh), v_s=vv[n_p:].reshape(nb_s, t_s, f_heads, f_dh),
                        lf_s=lf[n_p:].reshape(nb_s, t_s, f_heads))

        x32, x16 = _mm_res_ln(y16, w_mix.astype(BF16), x32, g[1], b[1], alpha=alpha, scale=1.0)
        x32, x16 = ffn(x32, x16, layer, 1, g[2], b[2])

    xp = x32[:n_p].reshape(nb_p, t, d)
    xs = x32[n_p:].reshape(nb_s, t_s, d)
    return (xp, xs, outs["c_p"], outs["n_p"], outs["m_p"], outs["c_s"], outs["n_s"], outs["m_s"],
            outs["k_p"], outs["v_p"], outs["lf_p"], outs["k_s"], outs["v_s"], outs["lf_s"])
```

```python
import functools
import math

import jax
import jax.numpy as jnp
from jax import lax
from jax.experimental import pallas as pl
from jax.experimental.pallas import tpu as pltpu

F32 = jnp.float32
BF16 = jnp.bfloat16
LN_EPS = 1e-5
NORM_EPS = 1e-6
LANES = 128
CHUNK = 128
PAGE = 128
FF_ALIGN = 512
DECODE_PAGES_PER_STEP = 8
VMEM_LIMIT = 56 * 1024 * 1024
NT_DIMS = (((1,), (1,)), ((), ()))
LOG2E = math.log2(math.e)


def _cparams(*sem):
    return pltpu.CompilerParams(dimension_semantics=sem, vmem_limit_bytes=VMEM_LIMIT)


def _pick(n, cands):
    for c in cands:
        if c <= n and n % c == 0:
            return c
    return n


def _log_sigmoid(x):
    return jnp.minimum(x, 0.0) - jnp.log1p(jnp.exp(-jnp.abs(x)))


def _dot01(x, m01):
    hi = x.astype(BF16)
    r1 = x - hi.astype(F32)
    mid = r1.astype(BF16)
    lo = (r1 - mid.astype(F32)).astype(BF16)
    d = lambda a: jnp.dot(a, m01, preferred_element_type=F32)
    return d(hi) + d(mid) + d(lo)


def _rows_to_cols(rows):
    r = rows.shape[0]
    if r < LANES:
        rows = jnp.concatenate([rows, jnp.zeros((LANES - r, LANES), F32)], axis=0)
    return rows.T


def _ffn_in_body(x_ref, wg_ref, wu_ref, h_ref, *, nf, shift):
    x = x_ref[...]
    g = jnp.dot(x, wg_ref[...].astype(BF16), preferred_element_type=F32)

    def finish(wu):
        u = jnp.dot(x, wu.astype(BF16), preferred_element_type=F32)
        h_ref[...] = (g * jax.nn.sigmoid(g) * u).astype(h_ref.dtype)

    lead0 = (0,) * (len(wu_ref.shape) - 2)
    if shift == 0:
        finish(wu_ref[lead0])
        return
    j = pl.program_id(1)

    @pl.when(j < nf - 1)
    def _():
        finish(wu_ref[lead0])

    @pl.when(j == nf - 1)
    def _():
        wu = wu_ref[lead0]
        finish(jnp.concatenate([wu[:, shift:], wu[:, :shift]], axis=1))


def _ffn_in(x16, w_in_all, lead):
    n, d = x16.shape
    f = w_in_all.shape[-1] // 2
    squeezed = (None,) * len(lead)
    tf = FF_ALIGN
    nf = -(-f // tf)
    shift = nf * tf - f
    assert shift % LANES == 0 and nf * tf <= 2 * f
    tm = _pick(n, (1664, 832, 640, 528, 512, 256, 128))
    body = functools.partial(_ffn_in_body, nf=nf, shift=shift)
    return pl.pallas_call(
        body,
        grid=(n // tm, nf),
        in_specs=[pl.BlockSpec((tm, d), lambda i, j: (i, 0)),
                  pl.BlockSpec(squeezed + (d, tf), lambda i, j: lead + (0, j)),
                  pl.BlockSpec(tuple(pl.Element(1) for _ in lead) + (pl.Element(d), pl.Element(tf)),
                               lambda i, j: lead + (0, LANES * jnp.minimum(f // LANES + j * (tf // LANES),
                                                                           (2 * f - tf) // LANES)))],
        out_specs=pl.BlockSpec((tm, tf), lambda i, j: (i, j)),
        out_shape=jax.ShapeDtypeStruct((n, nf * tf), BF16),
        compiler_params=_cparams("parallel", "arbitrary"),
        name="ffn_in",
    )(x16, w_in_all, w_in_all)


def _mm_res_ln_body(a_ref, w_ref, res_ref, g_ref, b_ref, o32_ref, o16_ref, *, alpha, scale):
    acc = jnp.dot(a_ref[...], w_ref[...], preferred_element_type=F32)
    z = alpha * res_ref[...] + scale * acc
    mu = jnp.mean(z, axis=-1, keepdims=True)
    zc = z - mu
    var = jnp.mean(zc * zc, axis=-1, keepdims=True)
    y = zc * lax.rsqrt(var + LN_EPS) * g_ref[...] + b_ref[...]
    o32_ref[...] = y
    o16_ref[...] = y.astype(BF16)


def _mm_res_ln(a16, w16, res32, g, b, *, alpha, scale, lead=()):
    n = a16.shape[0]
    kdim, d = w16.shape[-2:]
    assert kdim % LANES == 0 and kdim <= a16.shape[1]
    tm = _pick(n, (320, 256, 128))
    body = functools.partial(_mm_res_ln_body, alpha=alpha, scale=scale)
    return pl.pallas_call(
        body,
        grid=(n // tm,),
        in_specs=[pl.BlockSpec((tm, kdim), lambda i: (i, 0)),
                  pl.BlockSpec((None,) * len(lead) + (kdim, d), lambda i: lead + (0, 0),
                               pipeline_mode=pl.Buffered(1)),
                  pl.BlockSpec((tm, d), lambda i: (i, 0)),
                  pl.BlockSpec((1, d), lambda i: (0, 0)),
                  pl.BlockSpec((1, d), lambda i: (0, 0))],
        out_specs=[pl.BlockSpec((tm, d), lambda i: (i, 0)),
                   pl.BlockSpec((tm, d), lambda i: (i, 0))],
        out_shape=[jax.ShapeDtypeStruct((n, d), F32), jax.ShapeDtypeStruct((n, d), BF16)],
        compiler_params=_cparams("parallel"),
        name="mm_res_ln",
    )(a16, w16, res32, g.reshape(1, d), b.reshape(1, d))


def _proj_body(x_ref, wt_ref, cs_ref, o_ref, *, n_norm_tiles, group):
    z = lax.dot_general(x_ref[...], wt_ref[...].astype(BF16), NT_DIMS, preferred_element_type=F32)
    cs = cs_ref[...]
    if n_norm_tiles == 0:
        o_ref[...] = z * cs
        return
    j = pl.program_id(1)

    @pl.when(j >= n_norm_tiles)
    def _():
        o_ref[...] = z * cs

    @pl.when(j < n_norm_tiles)
    def _():
        for s in range(z.shape[1] // group):
            sl = slice(s * group, (s + 1) * group)
            zh = z[:, sl]
            ms = jnp.mean(zh * zh, axis=-1, keepdims=True)
            o_ref[:, sl] = zh * lax.rsqrt(ms + NORM_EPS) * cs[:, sl]


def _proj(x16, wt, colscale, *, norm_cols=0, group=LANES):
    n, d = x16.shape
    e = colscale.shape[0]
    tm = _pick(n, (1664, 832, 640, 528, 512, 256, 128))
    tn = _pick(e, (512, 256, 128))
    body = functools.partial(_proj_body, n_norm_tiles=norm_cols // tn, group=group)
    return pl.pallas_call(
        body,
        grid=(n // tm, e // tn),
        in_specs=[pl.BlockSpec((tm, d), lambda i, j: (i, 0)),
                  pl.BlockSpec((tn, d), lambda i, j: (j, 0)),
                  pl.BlockSpec((1, tn), lambda i, j: (0, j))],
        out_specs=pl.BlockSpec((tm, tn), lambda i, j: (i, j)),
        out_shape=jax.ShapeDtypeStruct((n, e), F32),
        compiler_params=_cparams("parallel", "arbitrary"),
        name="proj",
    )(x16, wt, colscale.reshape(1, e))


def _gate_body(x_ref, wc_ref, bc_ref, oc_ref, or_ref, *, ls_from, n_gates):
    zc = jnp.dot(x_ref[...], wc_ref[...], preferred_element_type=F32) + bc_ref[...]
    lane = lax.broadcasted_iota(jnp.int32, zc.shape, 1)
    act = jnp.where((lane >= ls_from) & (lane < n_gates), _log_sigmoid(zc), zc)
    oc_ref[...] = act
    for r in range(zc.shape[0] // LANES):
        or_ref[:, r * LANES:(r + 1) * LANES] = act[r * LANES:(r + 1) * LANES, :].T[:n_gates]


def _gates(x16, w_gate, bias, *, ls_from):
    n, d = x16.shape
    ng = w_gate.shape[1]
    assert ng % 8 == 0 and ng <= LANES
    tm = _pick(n, (1664, 1280, 1024, 640, 512, 256, 128))
    assert tm % LANES == 0
    wc = jnp.pad(w_gate, ((0, 0), (0, LANES - ng))).astype(BF16)
    bc = jnp.pad(bias.astype(F32), (0, LANES - ng)).reshape(1, LANES)
    body = functools.partial(_gate_body, ls_from=ls_from, n_gates=ng)
    return pl.pallas_call(
        body,
        grid=(n // tm,),
        in_specs=[pl.BlockSpec((tm, d), lambda i: (i, 0)),
                  pl.BlockSpec((d, LANES), lambda i: (0, 0)),
                  pl.BlockSpec((1, LANES), lambda i: (0, 0))],
        out_specs=[pl.BlockSpec((tm, LANES), lambda i: (i, 0)),
                   pl.BlockSpec((ng, tm), lambda i: (0, i))],
        out_shape=[jax.ShapeDtypeStruct((n, LANES), F32), jax.ShapeDtypeStruct((ng, n), F32)],
        compiler_params=_cparams("parallel"),
        name="gates",
    )(x16, wc, bc)


def _mlstm_body(q_ref, k_ref, v_ref, o_ref, g_ref, c0_ref, n0_ref, m0_ref, nw_ref,
                h_ref, c_ref, n_ref, m_ref, *, heads, dqk, dv, seg_shift, per_batch_seg):
    b = pl.program_id(0)
    c = pl.program_id(1)
    L = CHUNK
    seg_len = 1 << seg_shift
    seg = b if per_batch_seg else 0

    @pl.when(c == 0)
    def _():
        c_ref[...] = c0_ref[...]
        n_ref[...] = n0_ref[...]
        m_ref[...] = m0_ref[...]

    if per_batch_seg:
        @pl.when(b == 0)
        def _():
            h_ref[...] = jnp.zeros(h_ref.shape, h_ref.dtype)

    li = lax.broadcasted_iota(jnp.int32, (1, L), 1)
    si = lax.broadcasted_iota(jnp.int32, (L, 1), 0)
    lseg = lax.shift_right_logical(li, seg_shift)
    sseg = lax.shift_right_logical(si, seg_shift)
    pair_ok = (lseg == sseg) & (li <= si)
    row_live = sseg == seg
    lane_live = lseg == seg
    last_lane = seg * seg_len + (seg_len - 1)
    cum01 = jnp.where((lax.shift_right_logical(si, seg_shift) == lseg) & (si <= li), 1.0, 0.0).astype(BF16)

    gates = g_ref[...]
    ig = gates[:heads]
    lf = gates[heads:]
    bcum = _dot01(lf, cum01)
    src = ig - bcum
    bcols = _rows_to_cols(bcum)

    for h in range(heads):
        b_col = bcols[:, h:h + 1]
        b_row = bcum[h:h + 1, :]
        m_prev = m_ref[h:h + 1, 0:1]
        dmat = jnp.where(pair_ok, b_col + src[h:h + 1, :], -jnp.inf)
        inter = b_col + m_prev
        m_t = jnp.maximum(jnp.max(dmat, axis=-1, keepdims=True), inter)
        w_intra = jnp.exp(dmat - m_t)
        w_inter = jnp.exp(inter - m_t)

        qh = q_ref[:, h * dqk:(h + 1) * dqk]
        q16 = qh.astype(BF16)
        k16 = k_ref[:, h * dqk:(h + 1) * dqk].astype(BF16)
        vh = v_ref[:, h * dv:(h + 1) * dv]
        v16 = vh.astype(BF16)
        c_st = c_ref[h]
        n_st = n_ref[h:h + 1, :]

        s = lax.dot_general(q16, k16, NT_DIMS, preferred_element_type=F32) * w_intra
        num = (w_inter * lax.dot_general(q16, c_st.astype(BF16), NT_DIMS, preferred_element_type=F32)
               + jnp.dot(s.astype(BF16), v16, preferred_element_type=F32))
        den = (w_inter * jnp.sum(qh * n_st, axis=-1, keepdims=True)
               + jnp.sum(s, axis=-1, keepdims=True))
        hv = num / jnp.maximum(jnp.abs(den), jnp.exp(-m_t))
        ms = jnp.mean(hv * hv, axis=-1, keepdims=True)
        hn = hv * lax.rsqrt(ms + NORM_EPS) * nw_ref[:, h * dv:(h + 1) * dv]
        out = (hn * jax.nn.sigmoid(o_ref[:, h * dv:(h + 1) * dv])).astype(h_ref.dtype)
        if per_batch_seg:
            out = jnp.where(row_live, out, h_ref[:, h * dv:(h + 1) * dv])
        h_ref[:, h * dv:(h + 1) * dv] = out

        b_last = jnp.sum(jnp.where(li == last_lane, b_row, 0.0), axis=-1, keepdims=True)
        dec = jnp.where(lane_live, b_last - b_row + ig[h:h + 1, :], -jnp.inf)
        m_new = jnp.maximum(b_last + m_prev, jnp.max(dec, axis=-1, keepdims=True))
        w_s = jnp.exp(dec - m_new)
        w_c = jnp.exp(b_last + m_prev - m_new)
        vw = (vh.T * w_s).astype(BF16)
        c_ref[h] = w_c * c_st + jnp.dot(vw, k16, preferred_element_type=F32)
        w_s8 = jnp.broadcast_to(w_s, (8, L)).astype(BF16)
        n_ref[h:h + 1, :] = w_c * n_st + jnp.dot(w_s8, k16, preferred_element_type=F32)[0:1]
        m_ref[h:h + 1, :] = jnp.broadcast_to(m_new, (1, LANES))


def _mlstm(z, grow, c0, n0, m0, norm_w, *, row_block0, n_chunks, seg_shift, per_batch_seg, heads, dqk, dv):
    nb = c0.shape[0]
    nqk = heads * dqk
    nv = heads * dv
    qk_blocks_before_v = (2 * nqk) // nv
    if per_batch_seg:
        rb = lambda b, c: row_block0
        n_rows = CHUNK
        out_rb = lambda b, c: 0
    else:
        rb = lambda b, c: row_block0 + b * n_chunks + c
        n_rows = nb * n_chunks * CHUNK
        out_rb = lambda b, c: b * n_chunks + c
    body = functools.partial(_mlstm_body, heads=heads, dqk=dqk, dv=dv, seg_shift=seg_shift,
                             per_batch_seg=per_batch_seg)
    return pl.pallas_call(
        body,
        grid=(nb, n_chunks),
        in_specs=[pl.BlockSpec((CHUNK, nqk), lambda b, c: (rb(b, c), 0)),
                  pl.BlockSpec((CHUNK, nqk), lambda b, c: (rb(b, c), 1)),
                  pl.BlockSpec((CHUNK, nv), lambda b, c: (rb(b, c), qk_blocks_before_v)),
                  pl.BlockSpec((CHUNK, nv), lambda b, c: (rb(b, c), qk_blocks_before_v + 1)),
                  pl.BlockSpec((2 * heads, CHUNK), lambda b, c: (0, rb(b, c))),
                  pl.BlockSpec((None, heads, dv, dqk), lambda b, c: (b, 0, 0, 0)),
                  pl.BlockSpec((None, heads, dqk), lambda b, c: (b, 0, 0)),
                  pl.BlockSpec((None, heads, LANES), lambda b, c: (b, 0, 0)),
                  pl.BlockSpec((1, nv), lambda b, c: (0, 0))],
        out_specs=[pl.BlockSpec((CHUNK, nv), lambda b, c: (out_rb(b, c), 0)),
                   pl.BlockSpec((None, heads, dv, dqk), lambda b, c: (b, 0, 0, 0)),
                   pl.BlockSpec((None, heads, dqk), lambda b, c: (b, 0, 0)),
                   pl.BlockSpec((None, heads, LANES), lambda b, c: (b, 0, 0))],
        out_shape=[jax.ShapeDtypeStruct((n_rows, nv), BF16),
                   jax.ShapeDtypeStruct((nb, heads, dv, dqk), F32),
                   jax.ShapeDtypeStruct((nb, heads, dqk), F32),
                   jax.ShapeDtypeStruct((nb, heads, LANES), F32)],
        compiler_params=_cparams("arbitrary", "arbitrary"),
        name="mlstm_sample" if per_batch_seg else "mlstm_prompt",
    )(z, z, z, z, grow, c0, n0, m0, norm_w.reshape(1, nv))


def _cumsum_body(lf_ref, o_ref):
    t = lf_ref.shape[1]
    si = lax.broadcasted_iota(jnp.int32, (LANES, LANES), 0)
    li = lax.broadcasted_iota(jnp.int32, (LANES, LANES), 1)
    incl = jnp.where(si <= li, 1.0, 0.0).astype(BF16)
    carry = jnp.zeros((lf_ref.shape[0], 1), F32)
    for r in range(t // LANES):
        blk = _dot01(lf_ref[:, r * LANES:(r + 1) * LANES], incl) + carry
        o_ref[:, r * LANES:(r + 1) * LANES] = blk
        carry = blk[:, LANES - 1:LANES]


def _cumsum_rows(lf_rows, nb, t):
    hh = lf_rows.shape[0]
    return pl.pallas_call(
        _cumsum_body,
        grid=(nb,),
        in_specs=[pl.BlockSpec((hh, t), lambda b: (0, b))],
        out_specs=pl.BlockSpec((None, hh, t), lambda b: (b, 0, 0)),
        out_shape=jax.ShapeDtypeStruct((nb, hh, t), F32),
        compiler_params=_cparams("parallel"),
        name="fox_cumsum",
    )(lf_rows)


def _fox_attn_body(q_ref, k_ref, v_ref, g_ref, cq_ref, ck_ref, o_ref, cqb_s, *, tq, nq, scale):
    qi = pl.program_id(2)
    for r in range(tq // LANES):
        row = cq_ref[:, r * LANES:(r + 1) * LANES]
        cqb_s[r * LANES:(r + 1) * LANES, :] = jnp.broadcast_to(row * LOG2E, (LANES, LANES)).T
    q16 = q_ref[...].astype(BF16)

    for j in range(nq):
        @pl.when(qi == j)
        def _(j=j):
            k16 = k_ref[0:(j + 1) * tq, :].astype(BF16)
            v16 = v_ref[0:(j + 1) * tq, :].astype(BF16)
            ck2 = ck_ref[:, 0:(j + 1) * tq] * LOG2E
            rr = lax.broadcasted_iota(jnp.int32, (LANES, LANES), 0)
            cc = lax.broadcasted_iota(jnp.int32, (LANES, LANES), 1)
            for c in range(tq // LANES):
                rows = slice(c * LANES, (c + 1) * LANES)
                n_keys = j * tq + (c + 1) * LANES
                s = lax.dot_general(q16[rows], k16[:n_keys], NT_DIMS, preferred_element_type=F32) * (scale * LOG2E)
                s = s + jnp.concatenate([cqb_s[rows, :]] * (n_keys // LANES), axis=1) - ck2[:, :n_keys]
                tail = jnp.where(cc <= rr, s[:, n_keys - LANES:], -jnp.inf)
                s = tail if n_keys == LANES else jnp.concatenate([s[:, :n_keys - LANES], tail], axis=1)
                m = jnp.max(s, axis=-1, keepdims=True)
                p = jnp.exp2(s - m)
                den = jnp.sum(p, axis=-1, keepdims=True)
                o = jnp.dot(p.astype(BF16), v16[:n_keys], preferred_element_type=F32) / den
                o_ref[rows, :] = (o * jax.nn.sigmoid(g_ref[rows, :])).astype(o_ref.dtype)


def _fox_attn(z, c3, *, nb, t, heads, dh):
    tq = _pick(t, (512, 256, 128))
    nq = t // tq
    scale = dh ** -0.5
    body = functools.partial(_fox_attn_body, tq=tq, nq=nq, scale=scale)
    return pl.pallas_call(
        body,
        grid=(nb, heads, nq),
        in_specs=[pl.BlockSpec((tq, dh), lambda b, h, qi: (b * nq + qi, h)),
                  pl.BlockSpec((t, dh), lambda b, h, qi: (b, heads + h)),
                  pl.BlockSpec((t, dh), lambda b, h, qi: (b, 2 * heads + h)),
                  pl.BlockSpec((tq, dh), lambda b, h, qi: (b * nq + qi, 3 * heads + h)),
                  pl.BlockSpec((None, 1, tq), lambda b, h, qi: (b * heads + h, 0, qi)),
                  pl.BlockSpec((None, 1, t), lambda b, h, qi: (b * heads + h, 0, 0))],
        out_specs=pl.BlockSpec((tq, dh), lambda b, h, qi: (b * nq + qi, h)),
        out_shape=jax.ShapeDtypeStruct((nb * t, heads * dh), BF16),
        scratch_shapes=[pltpu.VMEM((tq, LANES), F32)],
        compiler_params=_cparams("parallel", "parallel", "arbitrary"),
        name="fox_attn",
    )(z, z, z, z, c3, c3)


def _dot01_left(m01, x):
    hi = x.astype(BF16)
    r1 = x - hi.astype(F32)
    mid = r1.astype(BF16)
    lo = (r1 - mid.astype(F32)).astype(BF16)
    d = lambda a: jnp.dot(m01, a, preferred_element_type=F32)
    return d(hi) + d(mid) + d(lo)


def _logf_suffix_body(lf_ref, r_ref, tot_ref, *, heads, rows):
    n = lf_ref.shape[0]
    pb = n // rows
    x = lf_ref[...]
    la = lax.broadcasted_iota(jnp.int32, (LANES, LANES), 0)
    lb = lax.broadcasted_iota(jnp.int32, (LANES, LANES), 1)
    same_head = (la % heads) == (lb % heads)
    later01 = jnp.where(same_head & (la > lb), 1.0, 0.0).astype(BF16)
    head01 = jnp.where(same_head, 1.0, 0.0).astype(BF16)
    within = _dot01(x, later01)
    row_tot = _dot01(x, head01)
    ra = lax.broadcasted_iota(jnp.int32, (LANES, LANES), 0)
    rb = lax.broadcasted_iota(jnp.int32, (LANES, LANES), 1)
    below01 = jnp.where((ra // rows == rb // rows) & (rb > ra), 1.0, 0.0).astype(BF16)
    for g in range(n // LANES):
        sl = slice(g * LANES, (g + 1) * LANES)
        r_ref[sl, :] = within[sl] + _dot01_left(below01, row_tot[sl])
    pa = lax.broadcasted_iota(jnp.int32, (pb, n), 0)
    pr = lax.broadcasted_iota(jnp.int32, (pb, n), 1)
    page01 = jnp.where(pr // rows == pa, 1.0, 0.0).astype(BF16)
    tot_ref[...] = _dot01_left(page01, row_tot)


def _logf_suffix(lf2, *, heads, rows):
    n_phys = lf2.shape[0] // rows
    pb = _pick(n_phys, (64, 32, 16, 8))
    assert (pb * rows) % LANES == 0 and LANES % rows == 0
    body = functools.partial(_logf_suffix_body, heads=heads, rows=rows)
    return pl.pallas_call(
        body,
        grid=(n_phys // pb,),
        in_specs=[pl.BlockSpec((pb * rows, LANES), lambda i: (i, 0))],
        out_specs=[pl.BlockSpec((pb * rows, LANES), lambda i: (i, 0)),
                   pl.BlockSpec((pb, LANES), lambda i: (i, 0))],
        out_shape=[jax.ShapeDtypeStruct((n_phys * rows, LANES), F32), jax.ShapeDtypeStruct((n_phys, LANES), F32)],
        compiler_params=_cparams("parallel"),
        name="logf_suffix",
    )(lf2)


def _fox_decode_body(pt_ref, q_ref, kn_ref, vn_ref, g_ref, lfn_ref, *rest, heads, n_new, n_steps, pages_per_step, scale):
    page_refs = rest[:4 * pages_per_step]
    o_ref, m_s, l_s, acc_s, carry_s, cn_s, mask_s, kn_s, vn_s = rest[4 * pages_per_step:]
    p = pl.program_id(1)
    nq = n_new * heads
    nkeys = page_refs[0].shape[0]
    rows_per_page = page_refs[2].shape[0]
    q16 = q_ref[...].astype(BF16)

    def update(lg, v16s):
        rep = lg.shape[1] // LANES
        width = lg.shape[1] // len(v16s)
        m_prev = m_s[...]
        m_new = jnp.maximum(m_prev, jnp.max(lg, axis=-1, keepdims=True))
        pp = jnp.exp(lg - jnp.concatenate([m_new] * rep, axis=1))
        corr = jnp.exp(m_prev - m_new)
        l_s[...] = corr * l_s[...] + jnp.sum(pp, axis=-1, keepdims=True)
        pv = jnp.dot(pp[:, :width].astype(BF16), v16s[0], preferred_element_type=F32)
        for i in range(1, len(v16s)):
            pv = pv + jnp.dot(pp[:, i * width:(i + 1) * width].astype(BF16), v16s[i], preferred_element_type=F32)
        acc_s[...] = corr * acc_s[...] + pv
        m_s[...] = m_new

    @pl.when(p == 0)
    def _():
        m_s[...] = jnp.full(m_s.shape, -jnp.inf, F32)
        l_s[...] = jnp.zeros(l_s.shape, F32)
        acc_s[...] = jnp.zeros(acc_s.shape, F32)
        carry_s[...] = jnp.zeros(carry_s.shape, F32)
        r_i = lax.broadcasted_iota(jnp.int32, (nq, nkeys), 0)
        c_i = lax.broadcasted_iota(jnp.int32, (nq, nkeys), 1)
        mask_s[...] = jnp.where((r_i % heads) == (c_i % heads), 0.0, -jnp.inf)
        ra = lax.broadcasted_iota(jnp.int32, (nq, nq), 0)
        rb = lax.broadcasted_iota(jnp.int32, (nq, nq), 1)
        upto01 = jnp.where(((ra % heads) == (rb % heads)) & (rb <= ra), 1.0, 0.0).astype(BF16)
        cn = _dot01_left(upto01, lfn_ref[...])
        cn_s[...] = cn
        kn_s[...] = jnp.zeros(kn_s.shape, F32)
        vn_s[...] = jnp.zeros(vn_s.shape, F32)
        kn_s[0:nq, :] = kn_ref[...]
        vn_s[0:nq, :] = vn_ref[...]
        s = lax.dot_general(q16, kn_s[...].astype(BF16), NT_DIMS, preferred_element_type=F32)
        cn_row = _rows_to_cols(cn)[0:1, :]
        rn = lax.broadcasted_iota(jnp.int32, (nq, LANES), 0)
        cc = lax.broadcasted_iota(jnp.int32, (nq, LANES), 1)
        ok = ((rn % heads) == (cc % heads)) & (cc <= rn) & (cc < nq)
        lg = jnp.where(ok, s * scale + cn - cn_row, -jnp.inf)
        update(lg, [vn_s[...].astype(BF16)])

    cn_full = jnp.concatenate([cn_s[...]] * rows_per_page, axis=1)
    carry = carry_s[0:1, :]
    lgs, v16s = [], []
    for i in range(pages_per_step):
        kp_ref, vp_ref, rl_ref, tot_ref = page_refs[4 * i:4 * i + 4]
        s = lax.dot_general(q16, kp_ref[...].astype(BF16), NT_DIMS, preferred_element_type=F32)
        bias = rl_ref[...] + carry
        bias_full = jnp.concatenate([jnp.broadcast_to(bias[j:j + 1, :], (nq, LANES)) for j in range(rows_per_page)],
                                    axis=1)
        lgs.append(s * scale + cn_full + bias_full + mask_s[...])
        v16s.append(vp_ref[...].astype(BF16))
        carry = carry + tot_ref[...]
    update(lgs[0] if pages_per_step == 1 else jnp.concatenate(lgs, axis=1), v16s)
    carry_s[...] = jnp.broadcast_to(carry, carry_s.shape)

    @pl.when(p == n_steps - 1)
    def _():
        o = acc_s[...] / l_s[...]
        o_ref[...] = (o * jax.nn.sigmoid(g_ref[...])).astype(o_ref.dtype)


def _fox_decode(q_r, kn_r, vn_r, g_r, lfn_r, cache_k2, cache_v2, r_local, tot, page_table, *, heads, n_new):
    nb, n_pages = page_table.shape
    nq, dh = q_r.shape[1:]
    nkeys = cache_k2.shape[1]
    rows = r_local.shape[1]
    assert dh == LANES and nq <= LANES and nq % 8 == 0
    scale = dh ** -0.5
    pps = DECODE_PAGES_PER_STEP if n_pages % DECODE_PAGES_PER_STEP == 0 else 1
    n_steps = n_pages // pps
    body = functools.partial(_fox_decode_body, heads=heads, n_new=n_new, n_steps=n_steps, pages_per_step=pps,
                             scale=scale)
    new_spec = pl.BlockSpec((None, nq, dh), lambda b, p, pt: (b, 0, 0))
    page_specs, page_args = [], []
    for i in range(pps):
        page = lambda b, p, pt, i=i: (pt[b, n_pages - 1 - (p * pps + i)], 0, 0)
        page_specs += [pl.BlockSpec((None, nkeys, dh), page), pl.BlockSpec((None, nkeys, dh), page),
                       pl.BlockSpec((None, rows, LANES), page), pl.BlockSpec((None, 1, LANES), page)]
        page_args += [cache_k2, cache_v2, r_local, tot]
    grid_spec = pltpu.PrefetchScalarGridSpec(
        num_scalar_prefetch=1,
        grid=(nb, n_steps),
        in_specs=[new_spec, new_spec, new_spec, new_spec, new_spec] + page_specs,
        out_specs=pl.BlockSpec((None, nq, dh), lambda b, p, pt: (b, 0, 0)),
        scratch_shapes=[pltpu.VMEM((nq, LANES), F32), pltpu.VMEM((nq, LANES), F32), pltpu.VMEM((nq, dh), F32),
                        pltpu.VMEM((8, LANES), F32), pltpu.VMEM((nq, LANES), F32), pltpu.VMEM((nq, nkeys), F32),
                        pltpu.VMEM((LANES, dh), F32), pltpu.VMEM((LANES, dh), F32)],
    )
    return pl.pallas_call(
        body,
        grid_spec=grid_spec,
        out_shape=jax.ShapeDtypeStruct((nb, nq, dh), BF16),
        compiler_params=_cparams("parallel", "arbitrary"),
        name="fox_decode",
    )(page_table, q_r, kn_r, vn_r, g_r, lfn_r, *page_args)


def kernel(x_prompt, x_sample, state_C, state_n, state_m, cache_k, cache_v, cache_logf, page_table,
           ln_g, ln_b, ffn_w_in, ffn_w_out, m_w_in, m_b_i, m_b_f, m_norm, m_w_out,
           f_w_in, f_b_f, f_q_norm, f_k_norm, f_w_out):
    nb_p, t, d = x_prompt.shape
    nb_s, t_s, _ = x_sample.shape
    depth = ln_g.shape[0]
    alpha = (2 * depth) ** 0.25
    n_p = nb_p * t
    n_s = nb_s * t_s
    assert n_s == CHUNK and t % CHUNK == 0 and (t_s & (t_s - 1)) == 0
    m_heads = m_b_i.shape[0]
    m_dqk = d // (2 * m_heads)
    m_dv = d // m_heads
    f_heads = f_b_f.shape[0]
    f_dh = d // f_heads
    nqk = m_heads * m_dqk
    nv = m_heads * m_dv

    x32 = jnp.concatenate([x_prompt.reshape(n_p, d), x_sample.reshape(n_s, d)], axis=0)
    x16 = x32.astype(BF16)

    ffn_w_out16 = ffn_w_out.astype(BF16)

    def ffn(x32, x16, layer, which, g, b):
        hid = _ffn_in(x16, ffn_w_in, (layer, which))
        return _mm_res_ln(hid, ffn_w_out16, x32, g, b, alpha=alpha, scale=0.5, lead=(layer, which))

    outs = {}
    for layer in range(depth):
        g, b = ln_g[layer], ln_b[layer]
        x32, x16 = ffn(x32, x16, layer, 0, g[0], b[0])

        if layer % 2 == 0:
            n_main = 2 * nqk + 2 * nv
            colscale = jnp.concatenate([jnp.ones((nqk,), F32), jnp.full((nqk,), m_dqk ** -0.5, F32),
                                        jnp.ones((2 * nv,), F32)])
            z = _proj(x16, m_w_in.T, colscale)
            _, grow = _gates(x16, m_w_in[:, n_main:], jnp.concatenate([m_b_i, m_b_f]), ls_from=m_heads)
            common = dict(heads=m_heads, dqk=m_dqk, dv=m_dv)
            zero_state = (jnp.zeros((nb_p, m_heads, m_dv, m_dqk), F32), jnp.zeros((nb_p, m_heads, m_dqk), F32),
                          jnp.zeros((nb_p, m_heads, LANES), F32))
            hp, c_p, n_pr, m_p = _mlstm(z, grow, *zero_state, m_norm, row_block0=0, n_chunks=t // CHUNK,
                                        seg_shift=int(math.log2(CHUNK)), per_batch_seg=False, **common)
            m0_s = jnp.broadcast_to(state_m.astype(F32)[:, :, None], (nb_s, m_heads, LANES))
            hs, c_s, n_sm, m_s = _mlstm(z, grow, state_C.astype(F32), state_n.astype(F32), m0_s, m_norm,
                                        row_block0=n_p // CHUNK, n_chunks=1, seg_shift=int(math.log2(t_s)),
                                        per_batch_seg=True, **common)
            y16 = jnp.concatenate([hp, hs], axis=0)
            w_mix = m_w_out
            outs.update(c_p=c_p, n_p=n_pr, m_p=m_p[:, :, 0], c_s=c_s, n_s=n_sm, m_s=m_s[:, :, 0])
        else:
            n_main = 4 * d
            colscale = jnp.concatenate([jnp.tile(f_q_norm.astype(F32), f_heads), jnp.tile(f_k_norm.astype(F32), f_heads),
                                        jnp.ones((2 * d,), F32)])
            z = _proj(x16, f_w_in.T, colscale, norm_cols=2 * d, group=f_dh)
            lf_cols, lf_rows = _gates(x16, f_w_in[:, n_main:], f_b_f, ls_from=0)
            c3 = _cumsum_rows(lf_rows, nb_p, t).reshape(nb_p * f_heads, 1, t)
            yp = _fox_attn(z, c3, nb=nb_p, t=t, heads=f_heads, dh=f_dh)
            n_phys, page_len = cache_k.shape[:2]
            rows = page_len * f_heads // LANES
            r_local, tot = _logf_suffix(cache_logf.astype(F32).reshape(n_phys * rows, LANES), heads=f_heads, rows=rows)
            zs = z[n_p:]
            per_row = lambda a: a.reshape(nb_s, t_s * f_heads, f_dh)
            lfn_r = jnp.broadcast_to(lf_cols[n_p:, :f_heads].reshape(nb_s, t_s * f_heads, 1),
                                     (nb_s, t_s * f_heads, LANES))
            ys = _fox_decode(per_row(zs[:, :d]), per_row(zs[:, d:2 * d]), per_row(zs[:, 2 * d:3 * d]),
                             per_row(zs[:, 3 * d:]), lfn_r,
                             cache_k.reshape(n_phys, page_len * f_heads, f_dh),
                             cache_v.reshape(n_phys, page_len * f_heads, f_dh),
                             r_local.reshape(n_phys, rows, LANES), tot.reshape(n_phys, 1, LANES), page_table,
                             heads=f_heads, n_new=t_s)
            y16 = jnp.concatenate([yp, ys.reshape(n_s, d)], axis=0)
            w_mix = f_w_out
            kk = z[:, d:2 * d]
            vv = z[:, 2 * d:3 * d]
            lf = lf_cols[:, :f_heads]
            outs.update(k_p=kk[:n_p].reshape(nb_p, t, f_heads, f_dh), v_p=vv[:n_p].reshape(nb_p, t, f_heads, f_dh),
                        lf_p=lf[:n_p].reshape(nb_p, t, f_heads),
                        k_s=kk[n_p:].reshape(nb_s, t_s, f_heads, f_dh), v_s=vv[n_p:].reshape(nb_s, t_s, f_heads, f_dh),
                        lf_s=lf[n_p:].reshape(nb_s, t_s, f_heads))

        x32, x16 = _mm_res_ln(y16, w_mix.astype(BF16), x32, g[1], b[1], alpha=alpha, scale=1.0)
        x32, x16 = ffn(x32, x16, layer, 1, g[2], b[2])

    xp = x32[:n_p].reshape(nb_p, t, d)
    xs = x32[n_p:].reshape(nb_s, t_s, d)
    return (xp, xs, outs["c_p"], outs["n_p"], outs["m_p"], outs["c_s"], outs["n_s"], outs["m_s"],
            outs["k_p"], outs["v_p"], outs["lf_p"], outs["k_s"], outs["v_s"], outs["lf_s"])
```
